```python
import jax, jax.numpy as jnp
from jax import lax
import numpy as np

D_MODEL = 4096
BATCH = 4
SEQ = 2048
DEPTH = 2
DEC_BATCH = 8
DEC_SEQ = 1
PAST_LEN = 16384
PAGE_SIZE = 128

N_A_LAYERS = DEPTH // 2
N_B_LAYERS = DEPTH - N_A_LAYERS
HGRN_EXPAND = 128
HGRN_HEADS = D_MODEL // HGRN_EXPAND
HGRN_DK = HGRN_EXPAND
HGRN_DV = D_MODEL // HGRN_HEADS
HGRN_CHUNK = 32
SB_HEAD_DIM = 128
SB_HEADS = D_MODEL // SB_HEAD_DIM
Q_BLOCK = 128
SB_BIAS_MEAN = -7.0
D_FF = ((8 * D_MODEL // 3 + 255) // 256) * 256
N_ADA = 9
EPS = 1e-6

kernel_name = "yoco_hgrn2_stickbreaking_macaron_step"


def _rms(x, g):
    x32 = x.astype(jnp.float32)
    return x32 * lax.rsqrt(jnp.mean(x32 * x32, axis=-1, keepdims=True) + EPS) * g.astype(jnp.float32)


def _modnorm(x, g, shift, scale):
    h = _rms(x, g) * (1.0 + scale[:, None, :].astype(jnp.float32)) + shift[:, None, :].astype(jnp.float32)
    return h.astype(x.dtype)


def _swiglu(h, w_up, w_down):
    gate, up = jnp.split(h @ w_up, 2, axis=-1)
    return (jax.nn.silu(gate) * up) @ w_down


def _hgrn2_recurrence(q, k, v, logf, s0):
    B, T, H, _ = q.shape
    C = min(HGRN_CHUNK, T)
    n = -(-T // C)
    pad = n * C - T

    def prep(a):
        a = jnp.pad(a.astype(jnp.float32), ((0, 0), (0, pad), (0, 0), (0, 0)))
        return a.reshape(B, n, C, H, a.shape[-1]).transpose(1, 0, 3, 2, 4)

    causal = jnp.tril(jnp.ones((C, C), dtype=bool))

    def step(S, inp):
        qc, kc, vc, gc = inp
        b = jnp.cumsum(gc, axis=2)
        o_inter = jnp.einsum('bhtd,bhde->bhte', qc * jnp.exp(b), S)
        diff = b[:, :, :, None, :] - b[:, :, None, :, :]
        decay = jnp.exp(jnp.where(causal[:, :, None], diff, -jnp.inf))
        scores = jnp.einsum('bhtd,bhsd,bhtsd->bhts', qc, kc, decay)
        o_intra = jnp.einsum('bhts,bhse->bhte', scores, vc)
        b_last = b[:, :, -1, :]
        S_new = jnp.exp(b_last)[..., None] * S + jnp.einsum(
            'bhsd,bhse->bhde', kc * jnp.exp(b_last[:, :, None, :] - b), vc)
        return S_new, o_inter + o_intra

    s_fin, o = lax.scan(step, s0.astype(jnp.float32), (prep(q), prep(k), prep(v), prep(logf)))
    o = o.transpose(1, 0, 3, 2, 4).reshape(B, n * C, H, -1)[:, :T]
    return o, s_fin


def _hgrn2_mixer(h, s0, w_in, lb, g_out_norm, w_out):
    B, T, _ = h.shape
    q, f, i, g = jnp.split(h @ w_in, 4, axis=-1)
    q = jax.nn.silu(q)
    f = lb + (1.0 - lb) * jax.nn.sigmoid(f.astype(jnp.float32))
    logf = jnp.log(f)
    k = 1.0 - f
    o, s_new = _hgrn2_recurrence(
        q.reshape(B, T, HGRN_HEADS, HGRN_DK), k.reshape(B, T, HGRN_HEADS, HGRN_DK),
        i.reshape(B, T, HGRN_HEADS, HGRN_DV), logf.reshape(B, T, HGRN_HEADS, HGRN_DK), s0)
    o = _rms(o, g_out_norm).reshape(B, T, D_MODEL).astype(h.dtype) * jax.nn.silu(g)
    return o @ w_out, s_new


def _stick_breaking(q, keys, values, q_pos, k_pos, logit_bias):
    B, Tq, H, Dh = q.shape
    scale = Dh ** -0.5
    bias = logit_bias.astype(jnp.float32)[None, :, None, None]

    def block(args):
        qb, pb = args
        z = jnp.concatenate([jnp.einsum('bqhd,bkhd->bhqk', qb, ks) for ks in keys],
                            axis=-1).astype(jnp.float32) * scale + bias
        mask = k_pos[None, :] < pb[:, None]
        log_keep = jnp.where(mask, jax.nn.log_sigmoid(-z), 0.0)
        tail = lax.cumsum(log_keep, axis=3, reverse=True) - log_keep
        w = jnp.where(mask, jnp.exp(jax.nn.log_sigmoid(z) + tail), 0.0).astype(values[0].dtype)
        outs = []
        off = 0
        for vs in values:
            n = vs.shape[1]
            outs.append(jnp.einsum('bhqk,bkhd->bqhd', w[..., off:off + n], vs))
            off += n
        return sum(outs)

    if Tq > Q_BLOCK and Tq % Q_BLOCK == 0:
        nb = Tq // Q_BLOCK
        qs = q.reshape(B, nb, Q_BLOCK, H, Dh).transpose(1, 0, 2, 3, 4)
        ps = q_pos.reshape(nb, Q_BLOCK)
        o = lax.map(block, (qs, ps))
        return o.transpose(1, 0, 2, 3, 4).reshape(B, Tq, H, Dh)
    return block((q, q_pos))


def _shared_kv(x, c, kv_ada_w, kv_ada_b, kv_norm_g, w_kv, k_norm_g):
    B, T, _ = x.shape
    shift, scale = jnp.split(jax.nn.silu(c) @ kv_ada_w + kv_ada_b, 2, axis=-1)
    h = _modnorm(x, kv_norm_g, shift, scale)
    k, v = jnp.split(h @ w_kv, 2, axis=-1)
    k = _rms(k.reshape(B, T, SB_HEADS, SB_HEAD_DIM), k_norm_g).astype(x.dtype)
    return k, v.reshape(B, T, SB_HEADS, SB_HEAD_DIM)


def _sb_mixer(h, keys, values, q_pos, k_pos, w_q, q_norm_g, logit_bias, w_o):
    B, T, _ = h.shape
    q = _rms((h @ w_q).reshape(B, T, SB_HEADS, SB_HEAD_DIM), q_norm_g).astype(h.dtype)
    o = _stick_breaking(q, keys, values, q_pos, k_pos, logit_bias)
    return o.reshape(B, T, D_MODEL) @ w_o


def _trunk(x, c, hgrn_s0, past_k, past_v,
           w_ada, b_ada, norm_g, ffn_w_up, ffn_w_down,
           hgrn_w_in, hgrn_lb, hgrn_out_norm_g, hgrn_w_out,
           kv_ada_w, kv_ada_b, kv_norm_g, w_kv, k_norm_g,
           sb_w_q, sb_q_norm_g, sb_logit_bias, sb_w_o):
    B, T, _ = x.shape
    past = 0 if past_k is None else past_k.shape[1]
    q_pos = past + jnp.arange(T, dtype=jnp.int32)
    k_pos = jnp.arange(past + T, dtype=jnp.int32)
    lower_bounds = jnp.cumsum(jax.nn.softmax(hgrn_lb.astype(jnp.float32), axis=0), axis=0)
    new_states = []
    k_new = None
    v_new = None
    for l in range(DEPTH):
        if l == N_A_LAYERS:
            k_new, v_new = _shared_kv(x, c, kv_ada_w, kv_ada_b, kv_norm_g, w_kv, k_norm_g)
        mod = jax.nn.silu(c) @ w_ada[l] + b_ada[l]
        sh1, sc1, gt1, shm, scm, gtm, sh2, sc2, gt2 = jnp.split(mod, N_ADA, axis=-1)
        x = x + 0.5 * gt1[:, None, :] * _swiglu(_modnorm(x, norm_g[l, 0], sh1, sc1),
                                                ffn_w_up[l, 0], ffn_w_down[l, 0])
        h = _modnorm(x, norm_g[l, 1], shm, scm)
        if l < N_A_LAYERS:
            o, s_new = _hgrn2_mixer(h, hgrn_s0[l], hgrn_w_in[l], lower_bounds[l],
                                    hgrn_out_norm_g[l], hgrn_w_out[l])
            new_states.append(s_new)
        else:
            j = l - N_A_LAYERS
            keys = (k_new,) if past_k is None else (past_k, k_new)
            vals = (v_new,) if past_v is None else (past_v, v_new)
            o = _sb_mixer(h, keys, vals, q_pos, k_pos, sb_w_q[j], sb_q_norm_g[j],
                          sb_logit_bias[j], sb_w_o[j])
        x = x + gtm[:, None, :] * o
        x = x + 0.5 * gt2[:, None, :] * _swiglu(_modnorm(x, norm_g[l, 2], sh2, sc2),
                                                ffn_w_up[l, 1], ffn_w_down[l, 1])
    return x, jnp.stack(new_states), k_new, v_new


def setup_inputs(seed: int = 0) -> dict:
    key = jax.random.key(seed)
    ks = jax.random.split(key, 32)
    f32 = jnp.float32
    n_pages = PAST_LEN // PAGE_SIZE
    n_phys = (DEC_BATCH * n_pages * 5) // 4

    def w(k, shape, fan_in, s=1.0):
        return jax.random.normal(k, shape, f32) * (s * fan_in ** -0.5)

    def gain(k, shape):
        return 1.0 + 0.02 * jax.random.normal(k, shape, f32)

    def bias(k, shape):
        return 0.01 * jax.random.normal(k, shape, f32)

    perm = jax.random.permutation(ks[5], n_phys).astype(jnp.int32)
    page_table = perm[:DEC_BATCH * n_pages].reshape(DEC_BATCH, n_pages)
    return {
        "x_prompt": jax.random.normal(ks[0], (BATCH, SEQ, D_MODEL), f32),
        "x_sample": jax.random.normal(ks[1], (DEC_BATCH, DEC_SEQ, D_MODEL), f32),
        "state_hgrn": 0.5 * jax.random.normal(ks[2], (N_A_LAYERS, DEC_BATCH, HGRN_HEADS, HGRN_DK, HGRN_DV), f32),
        "cache_k": jax.random.normal(ks[3], (n_phys, PAGE_SIZE, SB_HEADS, SB_HEAD_DIM), f32),
        "cache_v": jax.random.normal(ks[4], (n_phys, PAGE_SIZE, SB_HEADS, SB_HEAD_DIM), f32),
        "page_table": page_table,
        "c_prompt": jax.random.normal(ks[6], (BATCH, D_MODEL), f32),
        "c_sample": jax.random.normal(ks[7], (DEC_BATCH, D_MODEL), f32),
        "w_ada": w(ks[8], (DEPTH, D_MODEL, N_ADA * D_MODEL), D_MODEL, 0.5),
        "b_ada": bias(ks[9], (DEPTH, N_ADA * D_MODEL)),
        "norm_g": gain(ks[10], (DEPTH, 3, D_MODEL)),
        "ffn_w_up": w(ks[11], (DEPTH, 2, D_MODEL, 2 * D_FF), D_MODEL),
        "ffn_w_down": w(ks[12], (DEPTH, 2, D_FF, D_MODEL), D_FF),
        "hgrn_w_in": w(ks[13], (N_A_LAYERS, D_MODEL, 4 * D_MODEL), D_MODEL),
        "hgrn_lb": 0.1 * jax.random.normal(ks[14], (N_A_LAYERS + 1, D_MODEL), f32),
        "hgrn_out_norm_g": gain(ks[15], (N_A_LAYERS, HGRN_DV)),
        "hgrn_w_out": w(ks[16], (N_A_LAYERS, D_MODEL, D_MODEL), D_MODEL),
        "kv_ada_w": w(ks[17], (D_MODEL, 2 * D_MODEL), D_MODEL, 0.5),
        "kv_ada_b": bias(ks[18], (2 * D_MODEL,)),
        "kv_norm_g": gain(ks[19], (D_MODEL,)),
        "w_kv": w(ks[20], (D_MODEL, 2 * D_MODEL), D_MODEL),
        "k_norm_g": gain(ks[21], (SB_HEAD_DIM,)),
        "sb_w_q": w(ks[22], (N_B_LAYERS, D_MODEL, D_MODEL), D_MODEL),
        "sb_q_norm_g": gain(ks[23], (N_B_LAYERS, SB_HEAD_DIM)),
        "sb_logit_bias": SB_BIAS_MEAN + jax.random.uniform(ks[25], (N_B_LAYERS, SB_HEADS), f32, -2.0, 2.0),
        "sb_w_o": w(ks[24], (N_B_LAYERS, D_MODEL, D_MODEL), D_MODEL),
    }


def reference(x_prompt, x_sample, state_hgrn, cache_k, cache_v, page_table, c_prompt, c_sample,
              w_ada, b_ada, norm_g, ffn_w_up, ffn_w_down,
              hgrn_w_in, hgrn_lb, hgrn_out_norm_g, hgrn_w_out,
              kv_ada_w, kv_ada_b, kv_norm_g, w_kv, k_norm_g,
              sb_w_q, sb_q_norm_g, sb_logit_bias, sb_w_o):
    weights = (w_ada, b_ada, norm_g, ffn_w_up, ffn_w_down,
               hgrn_w_in, hgrn_lb, hgrn_out_norm_g, hgrn_w_out,
               kv_ada_w, kv_ada_b, kv_norm_g, w_kv, k_norm_g,
               sb_w_q, sb_q_norm_g, sb_logit_bias, sb_w_o)
    s0_prompt = jnp.zeros((N_A_LAYERS, x_prompt.shape[0], HGRN_HEADS, HGRN_DK, HGRN_DV), jnp.float32)
    y_prompt, state_hgrn_prompt, k_prompt, v_prompt = _trunk(
        x_prompt, c_prompt, s0_prompt, None, None, *weights)
    n_seq, n_pages = page_table.shape
    past_len = n_pages * cache_k.shape[1]
    past_k = cache_k[page_table].reshape(n_seq, past_len, SB_HEADS, SB_HEAD_DIM)
    past_v = cache_v[page_table].reshape(n_seq, past_len, SB_HEADS, SB_HEAD_DIM)
    y_sample, state_hgrn_sample, k_sample, v_sample = _trunk(
        x_sample, c_sample, state_hgrn, past_k, past_v, *weights)
    return (y_prompt, y_sample, state_hgrn_prompt, state_hgrn_sample, k_prompt, v_prompt, k_sample, v_sample)
```

```python
import functools
from typing import NamedTuple

import jax
import jax.numpy as jnp
from jax import lax
from jax.experimental import pallas as pl
from jax.experimental.pallas import tpu as pltpu

F32 = jnp.float32
BF16 = jnp.bfloat16
EPS = 1e-6
HEAD = 128
SUB = 8
ROWS_BF16 = 16
VMEM_LIMIT = 56 * 1024 * 1024

_NT = (((1,), (1,)), ((), ()))
_TN = (((0,), (0,)), ((), ()))


def _pick_tile(n, pref, align):
    if n <= pref:
        return n
    t = (pref // align) * align
    while t >= align:
        if n % t == 0:
            return t
        t -= align
    raise ValueError(f"no tile for {n} (pref {pref}, align {align})")


class _Group(NamedTuple):
    rows: int
    seq: int
    tm: int
    tr: int


def _silu(x):
    return x * (1.0 / (1.0 + jnp.exp(-x)))


def _sigmoid(x):
    return 1.0 / (1.0 + jnp.exp(-x))


def _log_sigmoid_neg(z):
    return -(jnp.maximum(z, 0.0) + jnp.log1p(jnp.exp(-jnp.abs(z))))


def _mm_kernel(*refs, nw, ne, no, nk, epilogue, prologue):
    a_ref = refs[0]
    w_refs = refs[1:1 + nw]
    e_refs = refs[1 + nw:1 + nw + ne]
    o_refs = refs[1 + nw + ne:1 + nw + ne + no]
    acc_refs = refs[1 + nw + ne + no:]
    a = a_ref[...]
    if prologue is not None:
        a = prologue(a)
    a = a.astype(BF16)
    prods = [jnp.dot(a, w_ref[...].astype(BF16), preferred_element_type=F32) for w_ref in w_refs]

    def finish(accs):
        vals = epilogue(accs, [e[...] for e in e_refs])
        for o_ref, v in zip(o_refs, vals):
            o_ref[...] = v.astype(o_ref.dtype)

    if nk == 1:
        finish(prods)
        return
    k = pl.program_id(2)

    @pl.when(k == 0)
    def _():
        for acc, p in zip(acc_refs, prods):
            acc[...] = p

    @pl.when(k > 0)
    def _():
        for acc, p in zip(acc_refs, prods):
            acc[...] += p

    @pl.when(k == nk - 1)
    def _():
        finish([acc[...] for acc in acc_refs])


def _matmul(name, a, w, *, n_cols, col_offsets=(0,), w_lead=(), tm, tn_pref, tk_pref,
            epilogue, out_dtypes, extras=(), prologue=None):
    m, kdim = a.shape
    tn = _pick_tile(n_cols, tn_pref, HEAD)
    tk = _pick_tile(kdim, tk_pref, HEAD)
    assert m % tm == 0, (m, tm)
    nm, nn, nk = m // tm, n_cols // tn, kdim // tk
    nw = len(col_offsets)
    w_lead = tuple(w_lead)
    assert w.ndim == 2 + len(w_lead)
    w_specs = []
    for off in col_offsets:
        assert off % tn == 0, (off, tn)
        w_specs.append(pl.BlockSpec((None,) * len(w_lead) + (tk, tn), functools.partial(
            lambda i, j, k, ob: w_lead + (k, j + ob), ob=off // tn)))
    in_specs = [pl.BlockSpec((tm, tk), lambda i, j, k: (i, k))] + w_specs
    in_specs += [pl.BlockSpec(bs, im) for _, bs, im in extras]
    out_specs = [pl.BlockSpec((tm, tn), lambda i, j, k: (i, j)) for _ in out_dtypes]
    out_shape = [jax.ShapeDtypeStruct((m, n_cols), dt) for dt in out_dtypes]
    scratch = [pltpu.VMEM((tm, tn), F32) for _ in range(nw)] if nk > 1 else []
    outs = pl.pallas_call(
        functools.partial(_mm_kernel, nw=nw, ne=len(extras), no=len(out_dtypes), nk=nk,
                          epilogue=epilogue, prologue=prologue),
        grid=(nm, nn, nk),
        in_specs=in_specs,
        out_specs=out_specs,
        out_shape=out_shape,
        scratch_shapes=scratch,
        compiler_params=pltpu.CompilerParams(
            dimension_semantics=("parallel", "parallel", "arbitrary"), vmem_limit_bytes=VMEM_LIMIT),
        name=name,
    )(a, *([w] * nw), *[e for e, _, _ in extras])
    return outs


def _mod_extra(grp, mod, piece, d, tn):
    assert d % tn == 0
    r = mod.shape[1]
    per = grp.seq // grp.tm
    ob = piece * d // tn
    return (mod, (None, r, tn), lambda i, j, k: (i // per, 0, j + ob))


def _row_extra(vec, tn):
    return (vec, (1, tn), lambda i, j, k: (0, j))


def _head_rms(x, gain):
    outs = []
    for g in range(x.shape[1] // HEAD):
        xg = x[:, g * HEAD:(g + 1) * HEAD]
        outs.append(xg * lax.rsqrt(jnp.mean(xg * xg, axis=-1, keepdims=True) + EPS) * gain)
    return outs[0] if len(outs) == 1 else jnp.concatenate(outs, axis=1)


def _modnorm_kernel(x_ref, g_ref, sh_ref, sc_ref, o_ref):
    x = x_ref[...]
    h = x * lax.rsqrt(jnp.mean(x * x, axis=-1, keepdims=True) + EPS) * g_ref[...]
    o_ref[...] = (h * (1.0 + sc_ref[...]) + sh_ref[...]).astype(o_ref.dtype)


def _modnorm(grp, x, gain, mod, shift_piece, scale_piece):
    m, d = x.shape
    r = mod.shape[1]
    per = grp.seq // grp.tr
    return pl.pallas_call(
        _modnorm_kernel,
        grid=(m // grp.tr,),
        in_specs=[
            pl.BlockSpec((grp.tr, d), lambda i: (i, 0)),
            pl.BlockSpec((1, d), lambda i: (0, 0)),
            pl.BlockSpec((None, r, d), lambda i: (i // per, 0, shift_piece)),
            pl.BlockSpec((None, r, d), lambda i: (i // per, 0, scale_piece)),
        ],
        out_specs=pl.BlockSpec((grp.tr, d), lambda i: (i, 0)),
        out_shape=jax.ShapeDtypeStruct((m, d), BF16),
        compiler_params=pltpu.CompilerParams(dimension_semantics=("parallel",), vmem_limit_bytes=VMEM_LIMIT),
        name="modnorm",
    )(x, gain.reshape(1, d), mod, mod)


def _hgrn_subchunk(qv, fv, vv, st, row):
    lg = jnp.log(fv)
    kv = 1.0 - fv
    b = lg
    for sh in (1, 2, 4):
        b = b + jnp.where(row >= sh, pltpu.roll(b, sh, axis=0), 0.0)
    blast = b[SUB - 1:SUB, :]
    qt = qv * jnp.exp(b)
    o = lax.dot_general(qt.astype(BF16), st.astype(BF16), _NT, preferred_element_type=F32)
    o = o + jnp.sum(qv * kv, axis=-1, keepdims=True) * vv
    for d in range(1, SUB):
        dec = jnp.where(row >= d, jnp.exp(b - pltpu.roll(b, d, axis=0)), 0.0)
        s = jnp.sum(qv * pltpu.roll(kv, d, axis=0) * dec, axis=-1, keepdims=True)
        o = o + s * pltpu.roll(vv, d, axis=0)
    kt = kv * jnp.exp(blast - b)
    u = lax.dot_general(vv.astype(BF16), kt.astype(BF16), _TN, preferred_element_type=F32)
    return o, jnp.exp(blast) * st + u


def _gated_head_norm(o, gv, gain):
    return o * lax.rsqrt(jnp.mean(o * o, axis=-1, keepdims=True) + EPS) * gain * _silu(gv)


def _hgrn_kernel(q_ref, f_ref, v_ref, g_ref, gn_ref, o_ref, s_ref, *, seq):
    row = lax.broadcasted_iota(jnp.int32, (SUB, HEAD), 0)
    gain = gn_ref[...]

    def body(n, st):
        r0 = pl.multiple_of(n * ROWS_BF16, ROWS_BF16)
        ys = []
        for half in range(ROWS_BF16 // SUB):
            rows = pl.ds(r0 + half * SUB, SUB)
            o, st = _hgrn_subchunk(q_ref[rows, :], f_ref[rows, :], v_ref[rows, :], st, row)
            ys.append(_gated_head_norm(o, g_ref[rows, :], gain))
        o_ref[pl.ds(r0, ROWS_BF16), :] = jnp.concatenate(ys, axis=0).astype(o_ref.dtype)
        return st

    st = lax.fori_loop(0, seq // ROWS_BF16, body, jnp.zeros((HEAD, HEAD), F32))
    s_ref[...] = st.T


def _hgrn_prompt(q, f, v, g, gain, batch, seq):
    m, d = q.shape
    heads = d // HEAD
    assert seq % ROWS_BF16 == 0
    spec = pl.BlockSpec((seq, HEAD), lambda b, h: (b, h))
    return pl.pallas_call(
        functools.partial(_hgrn_kernel, seq=seq),
        grid=(batch, heads),
        in_specs=[spec, spec, spec, spec, pl.BlockSpec((1, HEAD), lambda b, h: (0, 0))],
        out_specs=[spec, pl.BlockSpec((None, None, HEAD, HEAD), lambda b, h: (b, h, 0, 0))],
        out_shape=[jax.ShapeDtypeStruct((m, d), BF16), jax.ShapeDtypeStruct((batch, heads, HEAD, HEAD), F32)],
        compiler_params=pltpu.CompilerParams(
            dimension_semantics=("parallel", "parallel"), vmem_limit_bytes=VMEM_LIMIT),
        name="hgrn_prompt",
    )(q, f, v, g, gain.reshape(1, HEAD))


def _hgrn_step_kernel(q_ref, f_ref, v_ref, g_ref, gn_ref, s0_ref, o_ref, s_ref, *, batch):
    gain = gn_ref[...]
    o_ref[...] = jnp.zeros_like(o_ref)

    def col(rowvec):
        return jnp.broadcast_to(rowvec, (HEAD, HEAD)).T

    for b in range(batch):
        qv, fv, vv, gv = (r[b:b + 1, :] for r in (q_ref, f_ref, v_ref, g_ref))
        s_new = col(fv) * s0_ref[b] + col(1.0 - fv) * vv
        s_ref[b] = s_new
        o = jnp.sum(col(qv) * s_new, axis=0, keepdims=True)
        o_ref[b:b + 1, :] = _gated_head_norm(o, gv, gain)


def _hgrn_step(q, f, v, g, gain, s0):
    rows, d = q.shape
    batch, heads = s0.shape[:2]
    spec = pl.BlockSpec((rows, HEAD), lambda h: (0, h))
    sspec = pl.BlockSpec((batch, None, HEAD, HEAD), lambda h: (0, h, 0, 0))
    return pl.pallas_call(
        functools.partial(_hgrn_step_kernel, batch=batch),
        grid=(heads,),
        in_specs=[spec, spec, spec, spec, pl.BlockSpec((1, HEAD), lambda h: (0, 0)), sspec],
        out_specs=[spec, sspec],
        out_shape=[jax.ShapeDtypeStruct((rows, d), F32), jax.ShapeDtypeStruct(s0.shape, F32)],
        compiler_params=pltpu.CompilerParams(dimension_semantics=("parallel",), vmem_limit_bytes=VMEM_LIMIT),
        name="hgrn_step",
    )(q, f, v, g, gain.reshape(1, HEAD), s0)


def _split3(x):
    hi = x.astype(BF16)
    r1 = x - hi.astype(F32)
    mid = r1.astype(BF16)
    lo = (r1 - mid.astype(F32)).astype(BF16)
    return hi, mid, lo


def _sb_prompt_kernel(q_ref, k_ref, v_ref, bias_ref, o_ref, *, seq, scale):
    tb = HEAD
    r_io = lax.broadcasted_iota(jnp.int32, (tb, tb), 0)
    c_io = lax.broadcasted_iota(jnp.int32, (tb, tb), 1)
    later = (r_io > c_io).astype(BF16)
    causal = c_io < r_io
    bias = bias_ref[...]

    def step(qb, j, run, acc, diagonal):
        rows = pl.ds(pl.multiple_of(j * tb, tb), tb)
        z = lax.dot_general(qb, k_ref[rows, :], _NT, preferred_element_type=F32) * scale + bias
        lk = _log_sigmoid_neg(z)
        lkm = jnp.where(causal, lk, 0.0) if diagonal else lk
        tail = sum(jnp.dot(p, later, preferred_element_type=F32) for p in _split3(lkm))
        w = jnp.exp(z + lk + tail + run)
        if diagonal:
            w = jnp.where(causal, w, 0.0)
        acc = acc + jnp.dot(w.astype(BF16), v_ref[rows, :], preferred_element_type=F32)
        return run + jnp.sum(lkm, axis=-1, keepdims=True), acc

    def row_block(r, carry):
        qrows = pl.ds(pl.multiple_of(r * tb, tb), tb)
        qb = q_ref[qrows, :]
        run, acc = step(qb, r, jnp.zeros((tb, 1), F32), jnp.zeros((tb, HEAD), F32), True)
        run, acc = lax.fori_loop(0, r, lambda jj, c: step(qb, r - 1 - jj, c[0], c[1], False), (run, acc))
        o_ref[qrows, :] = acc.astype(o_ref.dtype)
        return carry

    lax.fori_loop(0, seq // tb, row_block, 0)


def _sb_prompt(q, k, v, bias, batch, seq):
    m, d = q.shape
    heads = d // HEAD
    assert seq % HEAD == 0
    spec = pl.BlockSpec((seq, HEAD), lambda b, h: (b, h))
    bias_row = jnp.repeat(bias.astype(F32), HEAD).reshape(1, d)
    return pl.pallas_call(
        functools.partial(_sb_prompt_kernel, seq=seq, scale=HEAD ** -0.5),
        grid=(batch, heads),
        in_specs=[spec, spec, spec, pl.BlockSpec((1, HEAD), lambda b, h: (0, h))],
        out_specs=spec,
        out_shape=jax.ShapeDtypeStruct((m, d), BF16),
        compiler_params=pltpu.CompilerParams(
            dimension_semantics=("parallel", "parallel"), vmem_limit_bytes=VMEM_LIMIT),
        name="sb_prompt",
    )(q, k, v, bias_row)


def _sb_sample_kernel(pt_ref, q_ref, k_ref, v_ref, bias_ref, o_ref, acc_ref, run_ref, *, heads, scale, n_pages):
    p = pl.program_id(1)
    page = k_ref.shape[0]

    @pl.when(p == 0)
    def _():
        acc_ref[...] = jnp.zeros_like(acc_ref)
        run_ref[...] = jnp.zeros_like(run_ref)

    z = sum(lax.dot_general(q_ref[h], k_ref[:, h, :].astype(BF16), _NT, preferred_element_type=F32)
            for h in range(heads))
    z = z * scale + bias_ref[...]
    lk = _log_sigmoid_neg(z)
    r_io = lax.broadcasted_iota(jnp.int32, (page, page), 0)
    c_io = lax.broadcasted_iota(jnp.int32, (page, page), 1)
    later = (r_io > c_io).astype(BF16)
    tail = sum(jnp.dot(part, later, preferred_element_type=F32) for part in _split3(lk))
    w = jnp.exp(z + lk + tail + run_ref[...]).astype(BF16)
    run_ref[...] += jnp.sum(lk, axis=-1, keepdims=True)
    head_io = lax.broadcasted_iota(jnp.int32, (heads, page), 0)
    acc_ref[...] += sum(
        jnp.dot(jnp.where(head_io == h, w, jnp.zeros_like(w)), v_ref[:, h, :].astype(BF16),
                preferred_element_type=F32)
        for h in range(heads))

    @pl.when(p == n_pages - 1)
    def _():
        o_ref[...] = acc_ref[...]


def _sb_sample(q, cache_k, cache_v, page_table, bias):
    batch, heads, _ = q.shape
    n_phys, page, _, _ = cache_k.shape
    n_pages = page_table.shape[1]
    qbd = q[:, :, None, :] * jnp.eye(heads, dtype=q.dtype)[None, :, :, None]
    bias_rows = jnp.broadcast_to(bias.astype(F32)[:, None], (heads, HEAD))
    kspec = pl.BlockSpec((None, page, heads, HEAD),
                         lambda b, p, pt: (pt[b * n_pages + n_pages - 1 - p], 0, 0, 0))
    return pl.pallas_call(
        functools.partial(_sb_sample_kernel, heads=heads, scale=HEAD ** -0.5, n_pages=n_pages),
        grid_spec=pltpu.PrefetchScalarGridSpec(
            num_scalar_prefetch=1,
            grid=(batch, n_pages),
            in_specs=[
                pl.BlockSpec((None, heads, heads, HEAD), lambda b, p, pt: (b, 0, 0, 0)),
                kspec, kspec,
                pl.BlockSpec((heads, HEAD), lambda b, p, pt: (0, 0)),
            ],
            out_specs=pl.BlockSpec((None, heads, HEAD), lambda b, p, pt: (b, 0, 0)),
            scratch_shapes=[pltpu.VMEM((heads, HEAD), F32), pltpu.VMEM((heads, HEAD), F32)],
        ),
        out_shape=jax.ShapeDtypeStruct((batch, heads, HEAD), F32),
        compiler_params=pltpu.CompilerParams(
            dimension_semantics=("parallel", "arbitrary"), vmem_limit_bytes=VMEM_LIMIT),
        name="sb_sample",
    )(page_table.reshape(-1).astype(jnp.int32), qbd, cache_k, cache_v, bias_rows)


def _ffn(grp, x, mod, pieces, gain, w_up, w_down, w_lead, d_ff):
    d = x.shape[1]
    h = _modnorm(grp, x, gain, mod, pieces[0], pieces[1])
    act, = _matmul("ffn_up", h, w_up, w_lead=w_lead, n_cols=d_ff, col_offsets=(0, d_ff), tm=grp.tm, tn_pref=256,
                   tk_pref=d, epilogue=lambda accs, ex: [_silu(accs[0]) * accs[1]], out_dtypes=[BF16])
    tn = _pick_tile(d, 512, HEAD)
    out, = _matmul("ffn_down", act, w_down, w_lead=w_lead, n_cols=d, tm=grp.tm, tn_pref=tn, tk_pref=5632,
                   epilogue=lambda accs, ex: [ex[0] + 0.5 * ex[1] * accs[0]], out_dtypes=[F32],
                   extras=[(x, (grp.tm, tn), lambda i, j, k: (i, j)), _mod_extra(grp, mod, pieces[2], d, tn)])
    return out


def _proj_residual(name, grp, a, w, w_lead, x, mod, gate_piece):
    d = x.shape[1]
    tn = _pick_tile(d, 512, HEAD)
    out, = _matmul(name, a, w, w_lead=w_lead, n_cols=d, tm=grp.tm, tn_pref=tn, tk_pref=a.shape[1],
                   epilogue=lambda accs, ex: [ex[0] + ex[1] * accs[0]], out_dtypes=[F32],
                   extras=[(x, (grp.tm, tn), lambda i, j, k: (i, j)), _mod_extra(grp, mod, gate_piece, d, tn)])
    return out


def _trunk(grp, x, mods, kv_mod, wts, hgrn_fn, attn_fn):
    d = x.shape[1]
    d_ff = wts["ffn_w_down"].shape[2]
    n_a = wts["hgrn_w_in"].shape[0]
    depth = wts["ffn_w_up"].shape[0]
    tn = _pick_tile(d, 512, HEAD)
    states = []
    k_new = v_new = k_bf = v_bf = None
    for l in range(depth):
        mod = mods[l]
        if l == n_a:
            h = _modnorm(grp, x, wts["kv_norm_g"], kv_mod, 0, 1)
            kgain = wts["k_norm_g"].reshape(1, HEAD)
            k_new, k_bf = _matmul(
                "kv_k", h, wts["w_kv"], n_cols=d, tm=grp.tm, tn_pref=tn, tk_pref=d,
                epilogue=lambda accs, ex: [_head_rms(accs[0], ex[0])] * 2, out_dtypes=[F32, BF16],
                extras=[(kgain, (1, HEAD), lambda i, j, k: (0, 0))])
            v_new, v_bf = _matmul(
                "kv_v", h, wts["w_kv"], n_cols=d, col_offsets=(d,), tm=grp.tm, tn_pref=tn, tk_pref=d,
                epilogue=lambda accs, ex: [accs[0]] * 2, out_dtypes=[F32, BF16])
        x = _ffn(grp, x, mod, (0, 1, 2), wts["norm_g"][l, 0], wts["ffn_w_up"], wts["ffn_w_down"], (l, 0), d_ff)
        h = _modnorm(grp, x, wts["norm_g"][l, 1], mod, 3, 4)
        if l < n_a:
            w_in = wts["hgrn_w_in"]
            lb = wts["lower_bounds"][l].reshape(1, d)
            q, = _matmul("hgrn_q", h, w_in, w_lead=(l,), n_cols=d, tm=grp.tm, tn_pref=tn, tk_pref=d,
                         epilogue=lambda accs, ex: [_silu(accs[0])], out_dtypes=[F32])
            f, = _matmul("hgrn_f", h, w_in, w_lead=(l,), n_cols=d, col_offsets=(d,), tm=grp.tm, tn_pref=tn, tk_pref=d,
                         epilogue=lambda accs, ex: [ex[0] + (1.0 - ex[0]) * _sigmoid(accs[0])],
                         out_dtypes=[F32], extras=[_row_extra(lb, tn)])
            v, = _matmul("hgrn_i", h, w_in, w_lead=(l,), n_cols=d, col_offsets=(2 * d,), tm=grp.tm, tn_pref=tn, tk_pref=d,
                         epilogue=lambda accs, ex: [accs[0]], out_dtypes=[F32])
            g, = _matmul("hgrn_g", h, w_in, w_lead=(l,), n_cols=d, col_offsets=(3 * d,), tm=grp.tm, tn_pref=tn, tk_pref=d,
                         epilogue=lambda accs, ex: [accs[0]], out_dtypes=[F32])
            o, s_new = hgrn_fn(l, q, f, v, g, wts["hgrn_out_norm_g"][l])
            states.append(s_new)
            x = _proj_residual("hgrn_out", grp, o, wts["hgrn_w_out"], (l,), x, mod, 5)
        else:
            j = l - n_a
            qgain = wts["sb_q_norm_g"][j].reshape(1, HEAD)
            q, = _matmul("sb_q", h, wts["sb_w_q"], w_lead=(j,), n_cols=d, tm=grp.tm, tn_pref=tn, tk_pref=d,
                         epilogue=lambda accs, ex: [_head_rms(accs[0], ex[0])], out_dtypes=[BF16],
                         extras=[(qgain, (1, HEAD), lambda i, j, k: (0, 0))])
            o = attn_fn(q, k_bf, v_bf, wts["sb_logit_bias"][j])
            x = _proj_residual("sb_out", grp, o, wts["sb_w_o"], (j,), x, mod, 5)
        x = _ffn(grp, x, mod, (6, 7, 8), wts["norm_g"][l, 2], wts["ffn_w_up"], wts["ffn_w_down"], (l, 1), d_ff)
    return x, jnp.stack(states), k_new, v_new


def _modulation(c_all, w, b, w_lead):
    n = b.shape[-1]
    out, = _matmul("adaln", c_all, w, n_cols=n, w_lead=w_lead, tm=c_all.shape[0], tn_pref=512,
                   tk_pref=c_all.shape[1], prologue=_silu,
                   epilogue=lambda accs, ex: [accs[0] + ex[0]], out_dtypes=[F32],
                   extras=[_row_extra(b.reshape(1, n), _pick_tile(n, 512, HEAD))])
    return out


def kernel(x_prompt, x_sample, state_hgrn, cache_k, cache_v, page_table, c_prompt, c_sample,
           w_ada, b_ada, norm_g, ffn_w_up, ffn_w_down, hgrn_w_in, hgrn_lb, hgrn_out_norm_g, hgrn_w_out,
           kv_ada_w, kv_ada_b, kv_norm_g, w_kv, k_norm_g, sb_w_q, sb_q_norm_g, sb_logit_bias, sb_w_o):
    batch, seq, d = x_prompt.shape
    dec_batch, dec_seq, _ = x_sample.shape
    assert dec_seq == 1 and d % HEAD == 0
    heads = d // HEAD
    depth = w_ada.shape[0]
    srows = -(-dec_batch // ROWS_BF16) * ROWS_BF16

    n_c = batch + dec_batch
    crows = -(-n_c // ROWS_BF16) * ROWS_BF16
    c_all = jnp.concatenate([c_prompt, c_sample, jnp.zeros((crows - n_c, d), F32)], axis=0)
    mod_all = [_modulation(c_all, w_ada, b_ada[l], (l,)) for l in range(depth)]
    kv_all = _modulation(c_all, kv_ada_w, kv_ada_b, ())

    def split(mod):
        pad = jnp.zeros((srows - dec_batch, mod.shape[1]), F32)
        return mod[:batch][:, None, :], jnp.concatenate([mod[batch:n_c], pad], axis=0)[None]

    mods_p, mods_s = zip(*[split(m) for m in mod_all])
    kv_p, kv_s = split(kv_all)

    wts = dict(
        norm_g=norm_g, ffn_w_up=ffn_w_up.astype(BF16), ffn_w_down=ffn_w_down.astype(BF16),
        hgrn_w_in=hgrn_w_in.astype(BF16), hgrn_out_norm_g=hgrn_out_norm_g, hgrn_w_out=hgrn_w_out.astype(BF16),
        kv_norm_g=kv_norm_g, w_kv=w_kv.astype(BF16), k_norm_g=k_norm_g, sb_w_q=sb_w_q.astype(BF16),
        sb_q_norm_g=sb_q_norm_g, sb_logit_bias=sb_logit_bias, sb_w_o=sb_w_o.astype(BF16),
        lower_bounds=jnp.cumsum(jax.nn.softmax(hgrn_lb.astype(F32), axis=0), axis=0),
    )

    grp_p = _Group(rows=batch * seq, seq=seq, tm=_pick_tile(seq, 1024, ROWS_BF16), tr=_pick_tile(seq, 256, ROWS_BF16))
    y_p, st_p, k_p, v_p = _trunk(
        grp_p, x_prompt.reshape(batch * seq, d), mods_p, kv_p, wts,
        lambda l, q, f, v, g, gain: _hgrn_prompt(q, f, v, g, gain, batch, seq),
        lambda q, k, v, bias: _sb_prompt(q, k, v, bias, batch, seq))

    grp_s = _Group(rows=srows, seq=srows, tm=srows, tr=srows)
    x_s = jnp.concatenate([x_sample.reshape(dec_batch, d), jnp.zeros((srows - dec_batch, d), F32)], axis=0)

    def attn_sample(q, k_bf, v_bf, bias):
        o = _sb_sample(q[:dec_batch].reshape(dec_batch, heads, HEAD), cache_k, cache_v, page_table, bias)
        return jnp.concatenate([o.reshape(dec_batch, d), jnp.zeros((srows - dec_batch, d), F32)], axis=0)

    y_s, st_s, k_s, v_s = _trunk(
        grp_s, x_s, mods_s, kv_s, wts,
        lambda l, q, f, v, g, gain: _hgrn_step(q, f, v, g, gain, state_hgrn[l]),
        attn_sample)

    return (y_p.reshape(batch, seq, d), y_s[:dec_batch].reshape(dec_batch, 1, d), st_p, st_s,
            k_p.reshape(batch, seq, heads, HEAD), v_p.reshape(batch, seq, heads, HEAD),
            k_s[:dec_batch].reshape(dec_batch, 1, heads, HEAD), v_s[:dec_batch].reshape(dec_batch, 1, heads, HEAD))
```

```python
import functools
import math
from typing import NamedTuple

import jax
import jax.numpy as jnp
from jax import lax
from jax.experimental import pallas as pl
from jax.experimental.pallas import tpu as pltpu

F32 = jnp.float32
BF16 = jnp.bfloat16
EPS = 1e-6
HEAD = 128
SUB = 8
ROWS_BF16 = 16
HGRN_CHUNK = 64
HGRN_UNROLL = 4
SB_HEADS_PER_STEP = 4
HGRN_FACTOR_MIN_F = math.exp(-70.0)
VMEM_LIMIT = 56 * 1024 * 1024

_NT = (((1,), (1,)), ((), ()))
_TN = (((0,), (0,)), ((), ()))


def _pick_tile(n, pref, align):
    if n <= pref:
        return n
    t = (pref // align) * align
    while t >= align:
        if n % t == 0:
            return t
        t -= align
    raise ValueError(f"no tile for {n} (pref {pref}, align {align})")


class _Group(NamedTuple):
    rows: int
    seq: int
    tm: int
    tr: int


def _silu(x):
    return x * (1.0 / (1.0 + jnp.exp(-x)))


def _sigmoid(x):
    return 1.0 / (1.0 + jnp.exp(-x))


def _log_sigmoid_neg(z):
    return -(jnp.maximum(z, 0.0) + jnp.log1p(jnp.exp(-jnp.abs(z))))


def _mm_kernel(*refs, nw, ne, no, nk, epilogue, prologue):
    a_ref = refs[0]
    w_refs = refs[1:1 + nw]
    e_refs = refs[1 + nw:1 + nw + ne]
    o_refs = refs[1 + nw + ne:1 + nw + ne + no]
    acc_refs = refs[1 + nw + ne + no:]
    a = a_ref[...]
    if prologue is not None:
        a = prologue(a)
    a = a.astype(BF16)
    prods = [jnp.dot(a, w_ref[...].astype(BF16), preferred_element_type=F32) for w_ref in w_refs]

    def finish(accs):
        vals = epilogue(accs, [e[...] for e in e_refs])
        for o_ref, v in zip(o_refs, vals):
            o_ref[...] = v.astype(o_ref.dtype)

    if nk == 1:
        finish(prods)
        return
    k = pl.program_id(2)

    @pl.when(k == 0)
    def _():
        for acc, p in zip(acc_refs, prods):
            acc[...] = p

    @pl.when(k > 0)
    def _():
        for acc, p in zip(acc_refs, prods):
            acc[...] += p

    @pl.when(k == nk - 1)
    def _():
        finish([acc[...] for acc in acc_refs])


def _matmul(name, a, w, *, n_cols, col_offsets=(0,), w_lead=(), tm, tn_pref, tk_pref,
            epilogue, out_dtypes, extras=(), prologue=None):
    m, kdim = a.shape
    tn = _pick_tile(n_cols, tn_pref, HEAD)
    tk = _pick_tile(kdim, tk_pref, HEAD)
    assert m % tm == 0, (m, tm)
    nm, nn, nk = m // tm, n_cols // tn, kdim // tk
    nw = len(col_offsets)
    w_lead = tuple(w_lead)
    assert w.ndim == 2 + len(w_lead)
    w_specs = []
    for off in col_offsets:
        assert off % tn == 0, (off, tn)
        w_specs.append(pl.BlockSpec((None,) * len(w_lead) + (tk, tn), functools.partial(
            lambda i, j, k, ob: w_lead + (k, j + ob), ob=off // tn)))
    in_specs = [pl.BlockSpec((tm, tk), lambda i, j, k: (i, k))] + w_specs
    in_specs += [pl.BlockSpec(bs, im) for _, bs, im in extras]
    out_specs = [pl.BlockSpec((tm, tn), lambda i, j, k: (i, j)) for _ in out_dtypes]
    out_shape = [jax.ShapeDtypeStruct((m, n_cols), dt) for dt in out_dtypes]
    scratch = [pltpu.VMEM((tm, tn), F32) for _ in range(nw)] if nk > 1 else []
    outs = pl.pallas_call(
        functools.partial(_mm_kernel, nw=nw, ne=len(extras), no=len(out_dtypes), nk=nk,
                          epilogue=epilogue, prologue=prologue),
        grid=(nm, nn, nk),
        in_specs=in_specs,
        out_specs=out_specs,
        out_shape=out_shape,
        scratch_shapes=scratch,
        compiler_params=pltpu.CompilerParams(
            dimension_semantics=("parallel", "parallel", "arbitrary"), vmem_limit_bytes=VMEM_LIMIT),
        name=name,
    )(a, *([w] * nw), *[e for e, _, _ in extras])
    return outs


def _mod_extra(grp, mod, piece, d, tn):
    assert d % tn == 0
    r = mod.shape[1]
    per = grp.seq // grp.tm
    ob = piece * d // tn
    return (mod, (None, r, tn), lambda i, j, k: (i // per, 0, j + ob))


def _row_extra(vec, tn):
    return (vec, (1, tn), lambda i, j, k: (0, j))


def _head_rms(x, gain):
    outs = []
    for g in range(x.shape[1] // HEAD):
        xg = x[:, g * HEAD:(g + 1) * HEAD]
        outs.append(xg * lax.rsqrt(jnp.mean(xg * xg, axis=-1, keepdims=True) + EPS) * gain)
    return outs[0] if len(outs) == 1 else jnp.concatenate(outs, axis=1)


def _modnorm_kernel(x_ref, g_ref, sh_ref, sc_ref, o_ref):
    x = x_ref[...]
    h = x * lax.rsqrt(jnp.mean(x * x, axis=-1, keepdims=True) + EPS) * g_ref[...]
    o_ref[...] = (h * (1.0 + sc_ref[...]) + sh_ref[...]).astype(o_ref.dtype)


def _modnorm(grp, x, gain, mod, shift_piece, scale_piece):
    m, d = x.shape
    r = mod.shape[1]
    per = grp.seq // grp.tr
    return pl.pallas_call(
        _modnorm_kernel,
        grid=(m // grp.tr,),
        in_specs=[
            pl.BlockSpec((grp.tr, d), lambda i: (i, 0)),
            pl.BlockSpec((1, d), lambda i: (0, 0)),
            pl.BlockSpec((None, r, d), lambda i: (i // per, 0, shift_piece)),
            pl.BlockSpec((None, r, d), lambda i: (i // per, 0, scale_piece)),
        ],
        out_specs=pl.BlockSpec((grp.tr, d), lambda i: (i, 0)),
        out_shape=jax.ShapeDtypeStruct((m, d), BF16),
        compiler_params=pltpu.CompilerParams(dimension_semantics=("parallel",), vmem_limit_bytes=VMEM_LIMIT),
        name="modnorm",
    )(x, gain.reshape(1, d), mod, mod)


def _hgrn_subchunk(qv, fv, vv, st, row):
    lg = jnp.log(fv)
    kv = 1.0 - fv
    b = lg
    for sh in (1, 2, 4):
        b = b + jnp.where(row >= sh, pltpu.roll(b, sh, axis=0), 0.0)
    blast = b[SUB - 1:SUB, :]
    qt = qv * jnp.exp(b)
    o = lax.dot_general(qt.astype(BF16), st.astype(BF16), _NT, preferred_element_type=F32)
    o = o + jnp.sum(qv * kv, axis=-1, keepdims=True) * vv
    for d in range(1, SUB):
        dec = jnp.where(row >= d, jnp.exp(b - pltpu.roll(b, d, axis=0)), 0.0)
        s = jnp.sum(qv * pltpu.roll(kv, d, axis=0) * dec, axis=-1, keepdims=True)
        o = o + s * pltpu.roll(vv, d, axis=0)
    kt = kv * jnp.exp(blast - b)
    u = lax.dot_general(vv.astype(BF16), kt.astype(BF16), _TN, preferred_element_type=F32)
    return o, jnp.exp(blast) * st + u


def _gated_head_norm(o, gv, gain):
    return o * lax.rsqrt(jnp.mean(o * o, axis=-1, keepdims=True) + EPS) * gain * _silu(gv)


def _hgrn_chunk(qv, fv, vv, st, lower, lower_ones):
    c = qv.shape[0]
    lg = jnp.log(fv)
    kv = 1.0 - fv
    b = sum(jnp.dot(lower_ones, part, preferred_element_type=F32) for part in _split3(lg))
    blast = b[c - 1:c, :]
    qt = (qv * jnp.exp(b)).astype(BF16)
    scores = lax.dot_general(qt, (kv * jnp.exp(-b)).astype(BF16), _NT, preferred_element_type=F32)
    scores = jnp.where(lower, scores, 0.0).astype(BF16)
    vb = vv.astype(BF16)
    o = lax.dot_general(qt, st.astype(BF16), _NT, preferred_element_type=F32)
    o = o + jnp.dot(scores, vb, preferred_element_type=F32)
    u = lax.dot_general(vb, (kv * jnp.exp(blast - b)).astype(BF16), _TN, preferred_element_type=F32)
    return o, jnp.exp(blast) * st + u


def _hgrn_kernel(q_ref, f_ref, v_ref, g_ref, gn_ref, o_ref, s_ref, *, seq, chunk):
    gain = gn_ref[...]
    zero_state = jnp.zeros((HEAD, HEAD), F32)
    fmin = jnp.min(f_ref[...])
    factorable = fmin >= HGRN_FACTOR_MIN_F ** (1.0 / chunk)

    @pl.when(factorable)
    def _():
        lower = (lax.broadcasted_iota(jnp.int32, (chunk, chunk), 1)
                 <= lax.broadcasted_iota(jnp.int32, (chunk, chunk), 0))
        lower_ones = lower.astype(BF16)

        def body(n, st):
            rows = pl.ds(pl.multiple_of(n * chunk, chunk), chunk)
            o, st = _hgrn_chunk(q_ref[rows, :], f_ref[rows, :], v_ref[rows, :], st, lower, lower_ones)
            o_ref[rows, :] = _gated_head_norm(o, g_ref[rows, :], gain).astype(o_ref.dtype)
            return st

        s_ref[...] = lax.fori_loop(0, seq // chunk, body, zero_state, unroll=HGRN_UNROLL).T

    @pl.when(jnp.logical_not(factorable))
    def _():
        row = lax.broadcasted_iota(jnp.int32, (SUB, HEAD), 0)

        def body(n, st):
            r0 = pl.multiple_of(n * ROWS_BF16, ROWS_BF16)
            ys = []
            for half in range(ROWS_BF16 // SUB):
                rows = pl.ds(r0 + half * SUB, SUB)
                o, st = _hgrn_subchunk(q_ref[rows, :], f_ref[rows, :], v_ref[rows, :], st, row)
                ys.append(_gated_head_norm(o, g_ref[rows, :], gain))
            o_ref[pl.ds(r0, ROWS_BF16), :] = jnp.concatenate(ys, axis=0).astype(o_ref.dtype)
            return st

        s_ref[...] = lax.fori_loop(0, seq // ROWS_BF16, body, zero_state).T


def _hgrn_prompt(q, f, v, g, gain, batch, seq):
    m, d = q.shape
    heads = d // HEAD
    chunk = _pick_tile(seq, HGRN_CHUNK, ROWS_BF16)
    assert seq % ROWS_BF16 == 0 and (seq // chunk) % HGRN_UNROLL == 0
    spec = pl.BlockSpec((seq, HEAD), lambda b, h: (b, h))
    return pl.pallas_call(
        functools.partial(_hgrn_kernel, seq=seq, chunk=chunk),
        grid=(batch, heads),
        in_specs=[spec, spec, spec, spec, pl.BlockSpec((1, HEAD), lambda b, h: (0, 0))],
        out_specs=[spec, pl.BlockSpec((None, None, HEAD, HEAD), lambda b, h: (b, h, 0, 0))],
        out_shape=[jax.ShapeDtypeStruct((m, d), BF16), jax.ShapeDtypeStruct((batch, heads, HEAD, HEAD), F32)],
        compiler_params=pltpu.CompilerParams(
            dimension_semantics=("parallel", "parallel"), vmem_limit_bytes=VMEM_LIMIT),
        name="hgrn_prompt",
    )(q, f, v, g, gain.reshape(1, HEAD))


def _hgrn_step_kernel(q_ref, f_ref, v_ref, g_ref, gn_ref, s0_ref, o_ref, s_ref, *, batch):
    gain = gn_ref[...]
    o_ref[...] = jnp.zeros_like(o_ref)

    def col(rowvec):
        return jnp.broadcast_to(rowvec, (HEAD, HEAD)).T

    for b in range(batch):
        qv, fv, vv, gv = (r[b:b + 1, :] for r in (q_ref, f_ref, v_ref, g_ref))
        s_new = col(fv) * s0_ref[b] + col(1.0 - fv) * vv
        s_ref[b] = s_new
        o = jnp.sum(col(qv) * s_new, axis=0, keepdims=True)
        o_ref[b:b + 1, :] = _gated_head_norm(o, gv, gain)


def _hgrn_step(q, f, v, g, gain, s0):
    rows, d = q.shape
    batch, heads = s0.shape[:2]
    spec = pl.BlockSpec((rows, HEAD), lambda h: (0, h))
    sspec = pl.BlockSpec((batch, None, HEAD, HEAD), lambda h: (0, h, 0, 0))
    return pl.pallas_call(
        functools.partial(_hgrn_step_kernel, batch=batch),
        grid=(heads,),
        in_specs=[spec, spec, spec, spec, pl.BlockSpec((1, HEAD), lambda h: (0, 0)), sspec],
        out_specs=[spec, sspec],
        out_shape=[jax.ShapeDtypeStruct((rows, d), F32), jax.ShapeDtypeStruct(s0.shape, F32)],
        compiler_params=pltpu.CompilerParams(dimension_semantics=("parallel",), vmem_limit_bytes=VMEM_LIMIT),
        name="hgrn_step",
    )(q, f, v, g, gain.reshape(1, HEAD), s0)


def _split3(x):
    hi = x.astype(BF16)
    r1 = x - hi.astype(F32)
    mid = r1.astype(BF16)
    lo = (r1 - mid.astype(F32)).astype(BF16)
    return hi, mid, lo


def _split2(x):
    hi = x.astype(BF16)
    return hi, (x - hi.astype(F32)).astype(BF16)


def _sb_prompt_kernel(q_ref, k_ref, v_ref, bias_ref, o_ref, acc_ref, run_ref, *, seq, scale, tq, hb):
    tb = HEAD
    nsub = tq // tb
    later = (lax.broadcasted_iota(jnp.int32, (tb, tb), 0)
             > lax.broadcasted_iota(jnp.int32, (tb, tb), 1)).astype(BF16)

    def step(hh, q0, lo, j, diagonal):
        n = tq - lo
        cols = slice(hh * HEAD, (hh + 1) * HEAD)
        krows = pl.ds(pl.multiple_of(j * tb, tb), tb)
        qb = q_ref[pl.ds(pl.multiple_of(q0 + lo, tb), n), cols]
        z = lax.dot_general(qb, k_ref[krows, cols], _NT, preferred_element_type=F32) * scale + bias_ref[:, cols]
        lk = _log_sigmoid_neg(z)
        if diagonal:
            causal = lax.broadcasted_iota(jnp.int32, (n, tb), 1) < lax.broadcasted_iota(jnp.int32, (n, tb), 0)
            lkm = jnp.where(causal, lk, 0.0)
        else:
            lkm = lk
        tail = sum(jnp.dot(part, later, preferred_element_type=F32) for part in _split2(lkm))
        w = jnp.exp(z + lk + tail + run_ref[hh, lo:, :])
        if diagonal:
            w = jnp.where(causal, w, 0.0)
        acc_ref[hh, lo:, :] += jnp.dot(w.astype(BF16), v_ref[krows, cols], preferred_element_type=F32)
        run_ref[hh, lo:, :] += jnp.sum(lkm, axis=-1, keepdims=True)

    def q_tile(r, carry):
        q0 = pl.multiple_of(r * tq, tq)
        acc_ref[...] = jnp.zeros_like(acc_ref)
        run_ref[...] = jnp.zeros_like(run_ref)
        for s in reversed(range(nsub)):
            for hh in range(hb):
                step(hh, q0, s * tb, r * nsub + s, True)

        def older(jj, c):
            for hh in range(hb):
                step(hh, q0, 0, r * nsub - 1 - jj, False)
            return c

        lax.fori_loop(0, r * nsub, older, 0)
        for hh in range(hb):
            o_ref[pl.ds(q0, tq), hh * HEAD:(hh + 1) * HEAD] = acc_ref[hh].astype(o_ref.dtype)
        return carry

    lax.fori_loop(0, seq // tq, q_tile, 0)


def _sb_prompt(q, k, v, bias, batch, seq):
    m, d = q.shape
    heads = d // HEAD
    tq = _pick_tile(seq, 512, HEAD)
    hb = _pick_tile(heads, SB_HEADS_PER_STEP, 1)
    assert seq % tq == 0
    spec = pl.BlockSpec((seq, hb * HEAD), lambda b, h: (b, h))
    bias_row = jnp.repeat(bias.astype(F32), HEAD).reshape(1, d)
    return pl.pallas_call(
        functools.partial(_sb_prompt_kernel, seq=seq, scale=HEAD ** -0.5, tq=tq, hb=hb),
        grid=(batch, heads // hb),
        in_specs=[spec, spec, spec, pl.BlockSpec((1, hb * HEAD), lambda b, h: (0, h))],
        out_specs=spec,
        out_shape=jax.ShapeDtypeStruct((m, d), BF16),
        scratch_shapes=[pltpu.VMEM((hb, tq, HEAD), F32), pltpu.VMEM((hb, tq, HEAD), F32)],
        compiler_params=pltpu.CompilerParams(
            dimension_semantics=("parallel", "parallel"), vmem_limit_bytes=VMEM_LIMIT),
        name="sb_prompt",
    )(q, k, v, bias_row)


def _sb_sample_kernel(pt_ref, q_ref, k_ref, v_ref, bias_ref, o_ref, acc_ref, run_ref, *, heads, scale, n_pages):
    p = pl.program_id(1)
    rows = k_ref.shape[0]
    groups = rows // HEAD
    shift = heads.bit_length() - 1

    @pl.when(p == 0)
    def _():
        acc_ref[...] = jnp.zeros_like(acc_ref)
        run_ref[...] = jnp.zeros_like(run_ref)

    def head_of(lane):
        return lane & (heads - 1)

    own = head_of(lax.broadcasted_iota(jnp.int32, (heads, HEAD), 1)) == lax.broadcasted_iota(
        jnp.int32, (heads, HEAD), 0)
    y = lax.dot_general(q_ref[...], k_ref[...].astype(BF16), _NT, preferred_element_type=F32)
    z = jnp.concatenate(
        [jnp.sum(jnp.where(own, y[:, j * HEAD:(j + 1) * HEAD], 0.0), axis=0, keepdims=True) for j in range(groups)],
        axis=0)
    z = z * scale + bias_ref[...]
    lk = _log_sigmoid_neg(z)

    src = lax.broadcasted_iota(jnp.int32, (HEAD, HEAD), 0)
    dst = lax.broadcasted_iota(jnp.int32, (HEAD, HEAD), 1)
    same_head = head_of(src) == head_of(dst)
    within_later = jnp.where(same_head, jnp.where((src >> shift) > (dst >> shift), 1.0, 0.0), 0.0).astype(BF16)
    same_head = jnp.where(same_head, 1.0, 0.0).astype(BF16)
    group_later = (lax.broadcasted_iota(jnp.int32, (groups, groups), 1)
                   > lax.broadcasted_iota(jnp.int32, (groups, groups), 0)).astype(BF16)
    parts = _split3(lk)
    within = sum(jnp.dot(part, within_later, preferred_element_type=F32) for part in parts)
    gsum = sum(jnp.dot(part, same_head, preferred_element_type=F32) for part in parts)
    beyond = sum(jnp.dot(group_later, part, preferred_element_type=F32) for part in _split3(gsum))
    w = jnp.exp(z + lk + within + beyond + run_ref[...])
    run_ref[...] += jnp.sum(gsum, axis=0, keepdims=True)
    wd = jnp.concatenate(
        [jnp.where(own, jnp.broadcast_to(w[j:j + 1, :], (heads, HEAD)), 0.0).astype(BF16)
         for j in range(groups)], axis=1)
    acc_ref[...] += jnp.dot(wd, v_ref[...].astype(BF16), preferred_element_type=F32)

    @pl.when(p == n_pages - 1)
    def _():
        o_ref[...] = acc_ref[...]


def _sb_sample(q, cache_k, cache_v, page_table, bias):
    batch, heads, _ = q.shape
    n_phys, page, _, _ = cache_k.shape
    n_pages = page_table.shape[1]
    assert heads & (heads - 1) == 0 and HEAD % heads == 0 and (page * heads) % HEAD == 0
    rows = page * heads
    bias_row = jnp.tile(bias.astype(F32), HEAD // heads).reshape(1, HEAD)
    kspec = pl.BlockSpec((None, rows, HEAD), lambda b, p, pt: (pt[b * n_pages + n_pages - 1 - p], 0, 0))
    return pl.pallas_call(
        functools.partial(_sb_sample_kernel, heads=heads, scale=HEAD ** -0.5, n_pages=n_pages),
        grid_spec=pltpu.PrefetchScalarGridSpec(
            num_scalar_prefetch=1,
            grid=(batch, n_pages),
            in_specs=[
                pl.BlockSpec((None, heads, HEAD), lambda b, p, pt: (b, 0, 0)),
                kspec, kspec,
                pl.BlockSpec((1, HEAD), lambda b, p, pt: (0, 0)),
            ],
            out_specs=pl.BlockSpec((None, heads, HEAD), lambda b, p, pt: (b, 0, 0)),
            scratch_shapes=[pltpu.VMEM((heads, HEAD), F32), pltpu.VMEM((1, HEAD), F32)],
        ),
        out_shape=jax.ShapeDtypeStruct((batch, heads, HEAD), F32),
        compiler_params=pltpu.CompilerParams(
            dimension_semantics=("parallel", "arbitrary"), vmem_limit_bytes=VMEM_LIMIT),
        name="sb_sample",
    )(page_table.reshape(-1).astype(jnp.int32), q, cache_k.reshape(n_phys, rows, HEAD),
      cache_v.reshape(n_phys, rows, HEAD), bias_row)


def _ffn(grp, x, mod, pieces, gain, w_up, w_down, w_lead, d_ff):
    d = x.shape[1]
    h = _modnorm(grp, x, gain, mod, pieces[0], pieces[1])
    act, = _matmul("ffn_up", h, w_up, w_lead=w_lead, n_cols=d_ff, col_offsets=(0, d_ff), tm=grp.tm, tn_pref=256,
                   tk_pref=d, epilogue=lambda accs, ex: [_silu(accs[0]) * accs[1]], out_dtypes=[BF16])
    tn = _pick_tile(d, 512, HEAD)
    out, = _matmul("ffn_down", act, w_down, w_lead=w_lead, n_cols=d, tm=grp.tm, tn_pref=tn, tk_pref=5632,
                   epilogue=lambda accs, ex: [ex[0] + 0.5 * ex[1] * accs[0]], out_dtypes=[F32],
                   extras=[(x, (grp.tm, tn), lambda i, j, k: (i, j)), _mod_extra(grp, mod, pieces[2], d, tn)])
    return out


def _proj_residual(name, grp, a, w, w_lead, x, mod, gate_piece):
    d = x.shape[1]
    tn = _pick_tile(d, 512, HEAD)
    out, = _matmul(name, a, w, w_lead=w_lead, n_cols=d, tm=grp.tm, tn_pref=tn, tk_pref=a.shape[1],
                   epilogue=lambda accs, ex: [ex[0] + ex[1] * accs[0]], out_dtypes=[F32],
                   extras=[(x, (grp.tm, tn), lambda i, j, k: (i, j)), _mod_extra(grp, mod, gate_piece, d, tn)])
    return out


def _trunk(grp, x, mods, kv_mod, wts, hgrn_fn, attn_fn):
    d = x.shape[1]
    d_ff = wts["ffn_w_down"].shape[2]
    n_a = wts["hgrn_w_in"].shape[0]
    depth = wts["ffn_w_up"].shape[0]
    tn = _pick_tile(d, 512, HEAD)
    states = []
    k_new = v_new = k_bf = v_bf = None
    for l in range(depth):
        mod = mods[l]
        if l == n_a:
            h = _modnorm(grp, x, wts["kv_norm_g"], kv_mod, 0, 1)
            kgain = wts["k_norm_g"].reshape(1, HEAD)
            k_new, k_bf = _matmul(
                "kv_k", h, wts["w_kv"], n_cols=d, tm=grp.tm, tn_pref=tn, tk_pref=d,
                epilogue=lambda accs, ex: [_head_rms(accs[0], ex[0])] * 2, out_dtypes=[F32, BF16],
                extras=[(kgain, (1, HEAD), lambda i, j, k: (0, 0))])
            v_new, v_bf = _matmul(
                "kv_v", h, wts["w_kv"], n_cols=d, col_offsets=(d,), tm=grp.tm, tn_pref=tn, tk_pref=d,
                epilogue=lambda accs, ex: [accs[0]] * 2, out_dtypes=[F32, BF16])
        x = _ffn(grp, x, mod, (0, 1, 2), wts["norm_g"][l, 0], wts["ffn_w_up"], wts["ffn_w_down"], (l, 0), d_ff)
        h = _modnorm(grp, x, wts["norm_g"][l, 1], mod, 3, 4)
        if l < n_a:
            w_in = wts["hgrn_w_in"]
            lb = wts["lower_bounds"][l].reshape(1, d)
            q, = _matmul("hgrn_q", h, w_in, w_lead=(l,), n_cols=d, tm=grp.tm, tn_pref=tn, tk_pref=d,
                         epilogue=lambda accs, ex: [_silu(accs[0])], out_dtypes=[F32])
            f, = _matmul("hgrn_f", h, w_in, w_lead=(l,), n_cols=d, col_offsets=(d,), tm=grp.tm, tn_pref=tn, tk_pref=d,
                         epilogue=lambda accs, ex: [ex[0] + (1.0 - ex[0]) * _sigmoid(accs[0])],
                         out_dtypes=[F32], extras=[_row_extra(lb, tn)])
            v, = _matmul("hgrn_i", h, w_in, w_lead=(l,), n_cols=d, col_offsets=(2 * d,), tm=grp.tm, tn_pref=tn, tk_pref=d,
                         epilogue=lambda accs, ex: [accs[0]], out_dtypes=[F32])
            g, = _matmul("hgrn_g", h, w_in, w_lead=(l,), n_cols=d, col_offsets=(3 * d,), tm=grp.tm, tn_pref=tn, tk_pref=d,
                         epilogue=lambda accs, ex: [accs[0]], out_dtypes=[F32])
            o, s_new = hgrn_fn(l, q, f, v, g, wts["hgrn_out_norm_g"][l])
            states.append(s_new)
            x = _proj_residual("hgrn_out", grp, o, wts["hgrn_w_out"], (l,), x, mod, 5)
        else:
            j = l - n_a
            qgain = wts["sb_q_norm_g"][j].reshape(1, HEAD)
            q, = _matmul("sb_q", h, wts["sb_w_q"], w_lead=(j,), n_cols=d, tm=grp.tm, tn_pref=tn, tk_pref=d,
                         epilogue=lambda accs, ex: [_head_rms(accs[0], ex[0])], out_dtypes=[BF16],
                         extras=[(qgain, (1, HEAD), lambda i, j, k: (0, 0))])
            o = attn_fn(q, k_bf, v_bf, wts["sb_logit_bias"][j])
            x = _proj_residual("sb_out", grp, o, wts["sb_w_o"], (j,), x, mod, 5)
        x = _ffn(grp, x, mod, (6, 7, 8), wts["norm_g"][l, 2], wts["ffn_w_up"], wts["ffn_w_down"], (l, 1), d_ff)
    return x, jnp.stack(states), k_new, v_new


def _modulation(c_all, w, b, w_lead):
    n = b.shape[-1]
    out, = _matmul("adaln", c_all, w, n_cols=n, w_lead=w_lead, tm=c_all.shape[0], tn_pref=512,
                   tk_pref=c_all.shape[1], prologue=_silu,
                   epilogue=lambda accs, ex: [accs[0] + ex[0]], out_dtypes=[F32],
                   extras=[_row_extra(b.reshape(1, n), _pick_tile(n, 512, HEAD))])
    return out


def kernel(x_prompt, x_sample, state_hgrn, cache_k, cache_v, page_table, c_prompt, c_sample,
           w_ada, b_ada, norm_g, ffn_w_up, ffn_w_down, hgrn_w_in, hgrn_lb, hgrn_out_norm_g, hgrn_w_out,
           kv_ada_w, kv_ada_b, kv_norm_g, w_kv, k_norm_g, sb_w_q, sb_q_norm_g, sb_logit_bias, sb_w_o):
    batch, seq, d = x_prompt.shape
    dec_batch, dec_seq, _ = x_sample.shape
    assert dec_seq == 1 and d % HEAD == 0
    heads = d // HEAD
    depth = w_ada.shape[0]
    srows = -(-dec_batch // ROWS_BF16) * ROWS_BF16

    n_c = batch + dec_batch
    crows = -(-n_c // ROWS_BF16) * ROWS_BF16
    c_all = jnp.concatenate([c_prompt, c_sample, jnp.zeros((crows - n_c, d), F32)], axis=0)
    mod_all = [_modulation(c_all, w_ada, b_ada[l], (l,)) for l in range(depth)]
    kv_all = _modulation(c_all, kv_ada_w, kv_ada_b, ())

    def split(mod):
        pad = jnp.zeros((srows - dec_batch, mod.shape[1]), F32)
        return mod[:batch][:, None, :], jnp.concatenate([mod[batch:n_c], pad], axis=0)[None]

    mods_p, mods_s = zip(*[split(m) for m in mod_all])
    kv_p, kv_s = split(kv_all)

    wts = dict(
        norm_g=norm_g, ffn_w_up=ffn_w_up.astype(BF16), ffn_w_down=ffn_w_down.astype(BF16),
        hgrn_w_in=hgrn_w_in.astype(BF16), hgrn_out_norm_g=hgrn_out_norm_g, hgrn_w_out=hgrn_w_out.astype(BF16),
        kv_norm_g=kv_norm_g, w_kv=w_kv.astype(BF16), k_norm_g=k_norm_g, sb_w_q=sb_w_q.astype(BF16),
        sb_q_norm_g=sb_q_norm_g, sb_logit_bias=sb_logit_bias, sb_w_o=sb_w_o.astype(BF16),
        lower_bounds=jnp.cumsum(jax.nn.softmax(hgrn_lb.astype(F32), axis=0), axis=0),
    )

    grp_p = _Group(rows=batch * seq, seq=seq, tm=_pick_tile(seq, 1024, ROWS_BF16), tr=_pick_tile(seq, 256, ROWS_BF16))
    y_p, st_p, k_p, v_p = _trunk(
        grp_p, x_prompt.reshape(batch * seq, d), mods_p, kv_p, wts,
        lambda l, q, f, v, g, gain: _hgrn_prompt(q, f, v, g, gain, batch, seq),
        lambda q, k, v, bias: _sb_prompt(q, k, v, bias, batch, seq))

    grp_s = _Group(rows=srows, seq=srows, tm=srows, tr=srows)
    x_s = jnp.concatenate([x_sample.reshape(dec_batch, d), jnp.zeros((srows - dec_batch, d), F32)], axis=0)

    def attn_sample(q, k_bf, v_bf, bias):
        o = _sb_sample(q[:dec_batch].reshape(dec_batch, heads, HEAD), cache_k, cache_v, page_table, bias)
        return jnp.concatenate([o.reshape(dec_batch, d), jnp.zeros((srows - dec_batch, d), F32)], axis=0)

    y_s, st_s, k_s, v_s = _trunk(
        grp_s, x_s, mods_s, kv_s, wts,
        lambda l, q, f, v, g, gain: _hgrn_step(q, f, v, g, gain, state_hgrn[l]),
        attn_sample)

    return (y_p.reshape(batch, seq, d), y_s[:dec_batch].reshape(dec_batch, 1, d), st_p, st_s,
            k_p.reshape(batch, seq, heads, HEAD), v_p.reshape(batch, seq, heads, HEAD),
            k_s[:dec_batch].reshape(dec_batch, 1, heads, HEAD), v_s[:dec_batch].reshape(dec_batch, 1, heads, HEAD))
```

```python
import functools
import math
from typing import NamedTuple

import jax
import jax.numpy as jnp
from jax import lax
from jax.experimental import pallas as pl
from jax.experimental.pallas import tpu as pltpu

F32 = jnp.float32
BF16 = jnp.bfloat16
EPS = 1e-6
HEAD = 128
SUB = 8
ROWS_BF16 = 16
HGRN_CHUNK = 64
HGRN_UNROLL = 4
SB_HEADS_PER_STEP = 4
SB_KEY_BLOCK = 256
HGRN_FACTOR_MIN_F = math.exp(-70.0)
VMEM_LIMIT = 56 * 1024 * 1024

_NT = (((1,), (1,)), ((), ()))
_TN = (((0,), (0,)), ((), ()))


def _pick_tile(n, pref, align):
    if n <= pref:
        return n
    t = (pref // align) * align
    while t >= align:
        if n % t == 0:
            return t
        t -= align
    raise ValueError(f"no tile for {n} (pref {pref}, align {align})")


class _Group(NamedTuple):
    rows: int
    seq: int
    tm: int
    tm_wide: int
    tn: int
    tr: int


def _silu(x):
    return x * (1.0 / (1.0 + jnp.exp(-x)))


def _sigmoid(x):
    return 1.0 / (1.0 + jnp.exp(-x))


def _log_sigmoid_neg(z):
    return -(jnp.maximum(z, 0.0) + jnp.log1p(jnp.exp(-jnp.abs(z))))


def _mm_kernel(*refs, nw, ne, no, nk, epilogue, prologue):
    a_ref = refs[0]
    w_refs = refs[1:1 + nw]
    e_refs = refs[1 + nw:1 + nw + ne]
    o_refs = refs[1 + nw + ne:1 + nw + ne + no]
    acc_refs = refs[1 + nw + ne + no:]
    a = a_ref[...]
    if prologue is not None:
        a = prologue(a)
    a = a.astype(BF16)
    prods = [jnp.dot(a, w_ref[...].astype(BF16), preferred_element_type=F32) for w_ref in w_refs]

    def finish(accs):
        vals = epilogue(accs, [e[...] for e in e_refs])
        for o_ref, v in zip(o_refs, vals):
            o_ref[...] = v.astype(o_ref.dtype)

    if nk == 1:
        finish(prods)
        return
    k = pl.program_id(2)

    @pl.when(k == 0)
    def _():
        for acc, p in zip(acc_refs, prods):
            acc[...] = p

    @pl.when(k > 0)
    def _():
        for acc, p in zip(acc_refs, prods):
            acc[...] += p

    @pl.when(k == nk - 1)
    def _():
        finish([acc[...] for acc in acc_refs])


def _matmul(name, a, w, *, n_cols, col_offsets=(0,), w_lead=(), tm, tn_pref, tk_pref,
            epilogue, out_dtypes, extras=(), prologue=None, a_single_buffer=False):
    m, kdim = a.shape
    tn = _pick_tile(n_cols, tn_pref, HEAD)
    tk = _pick_tile(kdim, tk_pref, HEAD)
    assert m % tm == 0, (m, tm)
    nm, nn, nk = m // tm, n_cols // tn, kdim // tk
    nw = len(col_offsets)
    w_lead = tuple(w_lead)
    assert w.ndim == 2 + len(w_lead)
    w_specs = []
    for off in col_offsets:
        assert off % tn == 0, (off, tn)
        w_specs.append(pl.BlockSpec((None,) * len(w_lead) + (tk, tn), functools.partial(
            lambda i, j, k, ob: w_lead + (k, j + ob), ob=off // tn)))
    a_mode = dict(pipeline_mode=pl.Buffered(1)) if (a_single_buffer and nk == 1) else {}
    in_specs = [pl.BlockSpec((tm, tk), lambda i, j, k: (i, k), **a_mode)] + w_specs
    in_specs += [pl.BlockSpec(bs, im) for _, bs, im in extras]
    out_specs = [pl.BlockSpec((tm, tn), lambda i, j, k: (i, j)) for _ in out_dtypes]
    out_shape = [jax.ShapeDtypeStruct((m, n_cols), dt) for dt in out_dtypes]
    scratch = [pltpu.VMEM((tm, tn), F32) for _ in range(nw)] if nk > 1 else []
    outs = pl.pallas_call(
        functools.partial(_mm_kernel, nw=nw, ne=len(extras), no=len(out_dtypes), nk=nk,
                          epilogue=epilogue, prologue=prologue),
        grid=(nm, nn, nk),
        in_specs=in_specs,
        out_specs=out_specs,
        out_shape=out_shape,
        scratch_shapes=scratch,
        compiler_params=pltpu.CompilerParams(
            dimension_semantics=("parallel", "parallel", "arbitrary"), vmem_limit_bytes=VMEM_LIMIT),
        name=name,
    )(a, *([w] * nw), *[e for e, _, _ in extras])
    return outs


def _mod_extra(grp, mod, piece, d, tn, tm):
    assert d % tn == 0
    r = mod.shape[1]
    per = grp.seq // tm
    ob = piece * d // tn
    return (mod, (None, r, tn), lambda i, j, k: (i // per, 0, j + ob))


def _row_extra(vec, tn):
    return (vec, (1, tn), lambda i, j, k: (0, j))


def _head_rms(x, gain):
    outs = []
    for g in range(x.shape[1] // HEAD):
        xg = x[:, g * HEAD:(g + 1) * HEAD]
        outs.append(xg * lax.rsqrt(jnp.mean(xg * xg, axis=-1, keepdims=True) + EPS) * gain)
    return outs[0] if len(outs) == 1 else jnp.concatenate(outs, axis=1)


def _modnorm_kernel(x_ref, g_ref, sh_ref, sc_ref, o_ref):
    x = x_ref[...]
    h = x * lax.rsqrt(jnp.mean(x * x, axis=-1, keepdims=True) + EPS) * g_ref[...]
    o_ref[...] = (h * (1.0 + sc_ref[...]) + sh_ref[...]).astype(o_ref.dtype)


def _modnorm(grp, x, gain, mod, shift_piece, scale_piece):
    m, d = x.shape
    r = mod.shape[1]
    per = grp.seq // grp.tr
    return pl.pallas_call(
        _modnorm_kernel,
        grid=(m // grp.tr,),
        in_specs=[
            pl.BlockSpec((grp.tr, d), lambda i: (i, 0)),
            pl.BlockSpec((1, d), lambda i: (0, 0)),
            pl.BlockSpec((None, r, d), lambda i: (i // per, 0, shift_piece)),
            pl.BlockSpec((None, r, d), lambda i: (i // per, 0, scale_piece)),
        ],
        out_specs=pl.BlockSpec((grp.tr, d), lambda i: (i, 0)),
        out_shape=jax.ShapeDtypeStruct((m, d), BF16),
        compiler_params=pltpu.CompilerParams(dimension_semantics=("parallel",), vmem_limit_bytes=VMEM_LIMIT),
        name="modnorm",
    )(x, gain.reshape(1, d), mod, mod)


def _hgrn_subchunk(qv, fv, vv, st, row):
    lg = jnp.log(fv)
    kv = 1.0 - fv
    b = lg
    for sh in (1, 2, 4):
        b = b + jnp.where(row >= sh, pltpu.roll(b, sh, axis=0), 0.0)
    blast = b[SUB - 1:SUB, :]
    qt = qv * jnp.exp(b)
    o = lax.dot_general(qt.astype(BF16), st.astype(BF16), _NT, preferred_element_type=F32)
    o = o + jnp.sum(qv * kv, axis=-1, keepdims=True) * vv
    for d in range(1, SUB):
        dec = jnp.where(row >= d, jnp.exp(b - pltpu.roll(b, d, axis=0)), 0.0)
        s = jnp.sum(qv * pltpu.roll(kv, d, axis=0) * dec, axis=-1, keepdims=True)
        o = o + s * pltpu.roll(vv, d, axis=0)
    kt = kv * jnp.exp(blast - b)
    u = lax.dot_general(vv.astype(BF16), kt.astype(BF16), _TN, preferred_element_type=F32)
    return o, jnp.exp(blast) * st + u


def _gated_head_norm(o, gv, gain):
    return o * lax.rsqrt(jnp.mean(o * o, axis=-1, keepdims=True) + EPS) * gain * _silu(gv)


def _hgrn_span(qv, fv, vv, st, lower, lower_ones, chunk):
    n = qv.shape[0]
    ends = [b1 * chunk - 1 for b1 in range(1, n // chunk + 1)]
    lg = jnp.log(fv)
    kv = 1.0 - fv
    b = sum(jnp.dot(lower_ones, part, preferred_element_type=F32) for part in _split3(lg))
    blast = jnp.concatenate([jnp.broadcast_to(b[e:e + 1, :], (chunk, HEAD)) for e in ends], axis=0)
    qt = (qv * jnp.exp(b)).astype(BF16)
    scores = lax.dot_general(qt, (kv * jnp.exp(-b)).astype(BF16), _NT, preferred_element_type=F32)
    scores = jnp.where(lower, scores, 0.0).astype(BF16)
    vb = vv.astype(BF16)
    kp = (kv * jnp.exp(blast - b)).astype(BF16)
    o_intra = jnp.dot(scores, vb, preferred_element_type=F32)
    outs = []
    for c, e in enumerate(ends):
        rows = slice(c * chunk, (c + 1) * chunk)
        outs.append(o_intra[rows] + lax.dot_general(qt[rows], st.astype(BF16), _NT, preferred_element_type=F32))
        u = lax.dot_general(vb[rows], kp[rows], _TN, preferred_element_type=F32)
        st = jnp.exp(b[e:e + 1, :]) * st + u
    return jnp.concatenate(outs, axis=0), st


def _hgrn_kernel(q_ref, f_ref, v_ref, g_ref, gn_ref, o_ref, s_ref, *, seq, chunk):
    gain = gn_ref[...]
    zero_state = jnp.zeros((HEAD, HEAD), F32)
    fmin = jnp.min(f_ref[...])
    factorable = fmin >= HGRN_FACTOR_MIN_F ** (1.0 / chunk)

    @pl.when(factorable)
    def _():
        span = chunk * HGRN_UNROLL
        t_io = lax.broadcasted_iota(jnp.int32, (span, span), 0)
        s_io = lax.broadcasted_iota(jnp.int32, (span, span), 1)
        shift = chunk.bit_length() - 1
        lower = (s_io <= t_io) & ((s_io >> shift) == (t_io >> shift))
        lower_ones = jnp.where(lower, 1.0, 0.0).astype(BF16)

        def body(n, st):
            rows = pl.ds(pl.multiple_of(n * span, span), span)
            o, st = _hgrn_span(q_ref[rows, :], f_ref[rows, :], v_ref[rows, :], st, lower, lower_ones, chunk)
            o_ref[rows, :] = _gated_head_norm(o, g_ref[rows, :], gain).astype(o_ref.dtype)
            return st

        s_ref[...] = lax.fori_loop(0, seq // span, body, zero_state).T

    @pl.when(jnp.logical_not(factorable))
    def _():
        row = lax.broadcasted_iota(jnp.int32, (SUB, HEAD), 0)

        def body(n, st):
            r0 = pl.multiple_of(n * ROWS_BF16, ROWS_BF16)
            ys = []
            for half in range(ROWS_BF16 // SUB):
                rows = pl.ds(r0 + half * SUB, SUB)
                o, st = _hgrn_subchunk(q_ref[rows, :], f_ref[rows, :], v_ref[rows, :], st, row)
                ys.append(_gated_head_norm(o, g_ref[rows, :], gain))
            o_ref[pl.ds(r0, ROWS_BF16), :] = jnp.concatenate(ys, axis=0).astype(o_ref.dtype)
            return st

        s_ref[...] = lax.fori_loop(0, seq // ROWS_BF16, body, zero_state).T


def _hgrn_prompt(q, f, v, g, gain, batch, seq):
    m, d = q.shape
    heads = d // HEAD
    chunk = _pick_tile(seq, HGRN_CHUNK, ROWS_BF16)
    assert seq % ROWS_BF16 == 0 and (seq // chunk) % HGRN_UNROLL == 0 and chunk & (chunk - 1) == 0
    spec = pl.BlockSpec((seq, HEAD), lambda b, h: (b, h))
    return pl.pallas_call(
        functools.partial(_hgrn_kernel, seq=seq, chunk=chunk),
        grid=(batch, heads),
        in_specs=[spec, spec, spec, spec, pl.BlockSpec((1, HEAD), lambda b, h: (0, 0))],
        out_specs=[spec, pl.BlockSpec((None, None, HEAD, HEAD), lambda b, h: (b, h, 0, 0))],
        out_shape=[jax.ShapeDtypeStruct((m, d), BF16), jax.ShapeDtypeStruct((batch, heads, HEAD, HEAD), F32)],
        compiler_params=pltpu.CompilerParams(
            dimension_semantics=("parallel", "parallel"), vmem_limit_bytes=VMEM_LIMIT),
        name="hgrn_prompt",
    )(q, f, v, g, gain.reshape(1, HEAD))


def _hgrn_step_kernel(q_ref, f_ref, v_ref, g_ref, gn_ref, s0_ref, o_ref, s_ref, *, batch):
    gain = gn_ref[...]
    o_ref[...] = jnp.zeros_like(o_ref)

    def col(rowvec):
        return jnp.broadcast_to(rowvec, (HEAD, HEAD)).T

    for b in range(batch):
        qv, fv, vv, gv = (r[b:b + 1, :] for r in (q_ref, f_ref, v_ref, g_ref))
        s_new = col(fv) * s0_ref[b] + col(1.0 - fv) * vv
        s_ref[b] = s_new
        o = jnp.sum(col(qv) * s_new, axis=0, keepdims=True)
        o_ref[b:b + 1, :] = _gated_head_norm(o, gv, gain)


def _hgrn_step(q, f, v, g, gain, s0):
    rows, d = q.shape
    batch, heads = s0.shape[:2]
    spec = pl.BlockSpec((rows, HEAD), lambda h: (0, h))
    sspec = pl.BlockSpec((batch, None, HEAD, HEAD), lambda h: (0, h, 0, 0))
    return pl.pallas_call(
        functools.partial(_hgrn_step_kernel, batch=batch),
        grid=(heads,),
        in_specs=[spec, spec, spec, spec, pl.BlockSpec((1, HEAD), lambda h: (0, 0)), sspec],
        out_specs=[spec, sspec],
        out_shape=[jax.ShapeDtypeStruct((rows, d), F32), jax.ShapeDtypeStruct(s0.shape, F32)],
        compiler_params=pltpu.CompilerParams(dimension_semantics=("parallel",), vmem_limit_bytes=VMEM_LIMIT),
        name="hgrn_step",
    )(q, f, v, g, gain.reshape(1, HEAD), s0)


def _split3(x):
    hi = x.astype(BF16)
    r1 = x - hi.astype(F32)
    mid = r1.astype(BF16)
    lo = (r1 - mid.astype(F32)).astype(BF16)
    return hi, mid, lo


def _split2(x):
    hi = x.astype(BF16)
    return hi, (x - hi.astype(F32)).astype(BF16)


def _sb_prompt_kernel(q_ref, k_ref, v_ref, bias_ref, o_ref, acc_ref, run_ref, *, seq, scale, tq, hb):
    tb = min(SB_KEY_BLOCK, tq)
    nsub = tq // tb
    later = (lax.broadcasted_iota(jnp.int32, (tb, tb), 0)
             > lax.broadcasted_iota(jnp.int32, (tb, tb), 1)).astype(BF16)

    def step(hh, q0, lo, j, diagonal):
        n = tq - lo
        cols = slice(hh * HEAD, (hh + 1) * HEAD)
        krows = pl.ds(pl.multiple_of(j * tb, tb), tb)
        qb = q_ref[pl.ds(pl.multiple_of(q0 + lo, tb), n), cols]
        z = lax.dot_general(qb, k_ref[krows, cols], _NT, preferred_element_type=F32) * scale
        z = z + bias_ref[:, hh * tb:(hh + 1) * tb]
        lk = _log_sigmoid_neg(z)
        if diagonal:
            causal = lax.broadcasted_iota(jnp.int32, (n, tb), 1) < lax.broadcasted_iota(jnp.int32, (n, tb), 0)
            lkm = jnp.where(causal, lk, 0.0)
        else:
            lkm = lk
        tail = sum(jnp.dot(part, later, preferred_element_type=F32) for part in _split2(lkm))
        w = jnp.exp(z + lk + tail + run_ref[hh, lo:, :])
        if diagonal:
            w = jnp.where(causal, w, 0.0)
        acc_ref[hh, lo:, :] += jnp.dot(w.astype(BF16), v_ref[krows, cols], preferred_element_type=F32)
        run_ref[hh, lo:, :] += jnp.sum(lkm, axis=-1, keepdims=True)

    def q_tile(r, carry):
        q0 = pl.multiple_of(r * tq, tq)
        acc_ref[...] = jnp.zeros_like(acc_ref)
        run_ref[...] = jnp.zeros_like(run_ref)
        for s in reversed(range(nsub)):
            for hh in range(hb):
                step(hh, q0, s * tb, r * nsub + s, True)

        def older(jj, c):
            for hh in range(hb):
                step(hh, q0, 0, r * nsub - 1 - jj, False)
            return c

        lax.fori_loop(0, r * nsub, older, 0)
        for hh in range(hb):
            o_ref[pl.ds(q0, tq), hh * HEAD:(hh + 1) * HEAD] = acc_ref[hh].astype(o_ref.dtype)
        return carry

    lax.fori_loop(0, seq // tq, q_tile, 0)


def _sb_prompt(q, k, v, bias, batch, seq):
    m, d = q.shape
    heads = d // HEAD
    tq = _pick_tile(seq, 512, HEAD)
    hb = _pick_tile(heads, SB_HEADS_PER_STEP, 1)
    assert seq % tq == 0
    spec = pl.BlockSpec((seq, hb * HEAD), lambda b, h: (b, h))
    tb = min(SB_KEY_BLOCK, tq)
    bias_row = jnp.repeat(bias.astype(F32), tb).reshape(1, heads * tb)
    return pl.pallas_call(
        functools.partial(_sb_prompt_kernel, seq=seq, scale=HEAD ** -0.5, tq=tq, hb=hb),
        grid=(batch, heads // hb),
        in_specs=[spec, spec, spec, pl.BlockSpec((1, hb * tb), lambda b, h: (0, h))],
        out_specs=spec,
        out_shape=jax.ShapeDtypeStruct((m, d), BF16),
        scratch_shapes=[pltpu.VMEM((hb, tq, HEAD), F32), pltpu.VMEM((hb, tq, tb), F32)],
        compiler_params=pltpu.CompilerParams(
            dimension_semantics=("parallel", "parallel"), vmem_limit_bytes=VMEM_LIMIT),
        name="sb_prompt",
    )(q, k, v, bias_row)


def _sb_sample_kernel(pt_ref, q_ref, k_ref, v_ref, bias_ref, o_ref, acc_ref, run_ref, *, heads, scale, n_pages):
    p = pl.program_id(1)
    rows = k_ref.shape[0]
    groups = rows // HEAD
    shift = heads.bit_length() - 1

    @pl.when(p == 0)
    def _():
        acc_ref[...] = jnp.zeros_like(acc_ref)
        run_ref[...] = jnp.zeros_like(run_ref)

    def head_of(lane):
        return lane & (heads - 1)

    own = head_of(lax.broadcasted_iota(jnp.int32, (heads, HEAD), 1)) == lax.broadcasted_iota(
        jnp.int32, (heads, HEAD), 0)
    y = lax.dot_general(q_ref[...], k_ref[...].astype(BF16), _NT, preferred_element_type=F32)
    z = jnp.concatenate(
        [jnp.sum(jnp.where(own, y[:, j * HEAD:(j + 1) * HEAD], 0.0), axis=0, keepdims=True) for j in range(groups)],
        axis=0)
    z = z * scale + bias_ref[...]
    lk = _log_sigmoid_neg(z)

    src = lax.broadcasted_iota(jnp.int32, (HEAD, HEAD), 0)
    dst = lax.broadcasted_iota(jnp.int32, (HEAD, HEAD), 1)
    same_head = head_of(src) == head_of(dst)
    within_later = jnp.where(same_head, jnp.where((src >> shift) > (dst >> shift), 1.0, 0.0), 0.0).astype(BF16)
    same_head = jnp.where(same_head, 1.0, 0.0).astype(BF16)
    group_later = (lax.broadcasted_iota(jnp.int32, (groups, groups), 1)
                   > lax.broadcasted_iota(jnp.int32, (groups, groups), 0)).astype(BF16)
    parts = _split3(lk)
    within = sum(jnp.dot(part, within_later, preferred_element_type=F32) for part in parts)
    gsum = sum(jnp.dot(part, same_head, preferred_element_type=F32) for part in parts)
    beyond = sum(jnp.dot(group_later, part, preferred_element_type=F32) for part in _split3(gsum))
    w = jnp.exp(z + lk + within + beyond + run_ref[...])
    run_ref[...] += jnp.sum(gsum, axis=0, keepdims=True)
    wd = jnp.concatenate(
        [jnp.where(own, jnp.broadcast_to(w[j:j + 1, :], (heads, HEAD)), 0.0).astype(BF16)
         for j in range(groups)], axis=1)
    acc_ref[...] += jnp.dot(wd, v_ref[...].astype(BF16), preferred_element_type=F32)

    @pl.when(p == n_pages - 1)
    def _():
        o_ref[...] = acc_ref[...]


def _sb_sample(q, cache_k, cache_v, page_table, bias):
    batch, heads, _ = q.shape
    n_phys, page, _, _ = cache_k.shape
    n_pages = page_table.shape[1]
    assert heads & (heads - 1) == 0 and HEAD % heads == 0 and (page * heads) % HEAD == 0
    rows = page * heads
    bias_row = jnp.tile(bias.astype(F32), HEAD // heads).reshape(1, HEAD)
    kspec = pl.BlockSpec((None, rows, HEAD), lambda b, p, pt: (pt[b * n_pages + n_pages - 1 - p], 0, 0))
    return pl.pallas_call(
        functools.partial(_sb_sample_kernel, heads=heads, scale=HEAD ** -0.5, n_pages=n_pages),
        grid_spec=pltpu.PrefetchScalarGridSpec(
            num_scalar_prefetch=1,
            grid=(batch, n_pages),
            in_specs=[
                pl.BlockSpec((None, heads, HEAD), lambda b, p, pt: (b, 0, 0)),
                kspec, kspec,
                pl.BlockSpec((1, HEAD), lambda b, p, pt: (0, 0)),
            ],
            out_specs=pl.BlockSpec((None, heads, HEAD), lambda b, p, pt: (b, 0, 0)),
            scratch_shapes=[pltpu.VMEM((heads, HEAD), F32), pltpu.VMEM((1, HEAD), F32)],
        ),
        out_shape=jax.ShapeDtypeStruct((batch, heads, HEAD), F32),
        compiler_params=pltpu.CompilerParams(
            dimension_semantics=("parallel", "arbitrary"), vmem_limit_bytes=VMEM_LIMIT),
        name="sb_sample",
    )(page_table.reshape(-1).astype(jnp.int32), q, cache_k.reshape(n_phys, rows, HEAD),
      cache_v.reshape(n_phys, rows, HEAD), bias_row)


def _ffn(grp, x, mod, pieces, gain, w_up, w_down, w_lead, d_ff):
    d = x.shape[1]
    h = _modnorm(grp, x, gain, mod, pieces[0], pieces[1])
    act, = _matmul("ffn_up", h, w_up, w_lead=w_lead, n_cols=d_ff, col_offsets=(0, d_ff), tm=grp.tm_wide,
                   tn_pref=256, tk_pref=d, epilogue=lambda accs, ex: [_silu(accs[0]) * accs[1]], out_dtypes=[BF16])
    tn = _pick_tile(d, 512, HEAD)
    out, = _matmul("ffn_down", act, w_down, w_lead=w_lead, n_cols=d, tm=grp.tm, tn_pref=tn, tk_pref=5632,
                   epilogue=lambda accs, ex: [ex[0] + 0.5 * ex[1] * accs[0]], out_dtypes=[F32],
                   extras=[(x, (grp.tm, tn), lambda i, j, k: (i, j)),
                           _mod_extra(grp, mod, pieces[2], d, tn, grp.tm)])
    return out


def _proj_residual(name, grp, a, w, w_lead, x, mod, gate_piece):
    d = x.shape[1]
    tn = _pick_tile(d, grp.tn, HEAD)
    out, = _matmul(name, a, w, w_lead=w_lead, n_cols=d, tm=grp.tm_wide, tn_pref=tn, tk_pref=a.shape[1],
                   epilogue=lambda accs, ex: [ex[0] + ex[1] * accs[0]], out_dtypes=[F32], a_single_buffer=True,
                   extras=[(x, (grp.tm_wide, tn), lambda i, j, k: (i, j)),
                           _mod_extra(grp, mod, gate_piece, d, tn, grp.tm_wide)])
    return out


def _trunk(grp, x, mods, kv_mod, wts, hgrn_fn, attn_fn):
    d = x.shape[1]
    d_ff = wts["ffn_w_down"].shape[2]
    n_a = wts["hgrn_w_in"].shape[0]
    depth = wts["ffn_w_up"].shape[0]
    tn = _pick_tile(d, grp.tn, HEAD)
    states = []
    k_new = v_new = k_bf = v_bf = None
    for l in range(depth):
        mod = mods[l]
        if l == n_a:
            h = _modnorm(grp, x, wts["kv_norm_g"], kv_mod, 0, 1)
            kgain = wts["k_norm_g"].reshape(1, HEAD)
            k_new, k_bf = _matmul(
                "kv_k", h, wts["w_kv"], n_cols=d, tm=grp.tm_wide, tn_pref=tn, tk_pref=d,
                epilogue=lambda accs, ex: [_head_rms(accs[0], ex[0])] * 2, out_dtypes=[F32, BF16],
                a_single_buffer=True, extras=[(kgain, (1, HEAD), lambda i, j, k: (0, 0))])
            v_new, v_bf = _matmul(
                "kv_v", h, wts["w_kv"], n_cols=d, col_offsets=(d,), tm=grp.tm_wide, tn_pref=tn, tk_pref=d,
                epilogue=lambda accs, ex: [accs[0]] * 2, out_dtypes=[F32, BF16], a_single_buffer=True)
        x = _ffn(grp, x, mod, (0, 1, 2), wts["norm_g"][l, 0], wts["ffn_w_up"], wts["ffn_w_down"], (l, 0), d_ff)
        h = _modnorm(grp, x, wts["norm_g"][l, 1], mod, 3, 4)
        if l < n_a:
            w_in = wts["hgrn_w_in"]
            lb = wts["lower_bounds"][l].reshape(1, d)
            q, = _matmul("hgrn_q", h, w_in, w_lead=(l,), n_cols=d, tm=grp.tm_wide, tn_pref=tn, tk_pref=d,
                         epilogue=lambda accs, ex: [_silu(accs[0])], out_dtypes=[F32])
            f, = _matmul("hgrn_f", h, w_in, w_lead=(l,), n_cols=d, col_offsets=(d,), tm=grp.tm_wide, tn_pref=tn, tk_pref=d,
                         epilogue=lambda accs, ex: [ex[0] + (1.0 - ex[0]) * _sigmoid(accs[0])],
                         out_dtypes=[F32], extras=[_row_extra(lb, tn)])
            v, = _matmul("hgrn_i", h, w_in, w_lead=(l,), n_cols=d, col_offsets=(2 * d,), tm=grp.tm_wide, tn_pref=tn, tk_pref=d,
                         epilogue=lambda accs, ex: [accs[0]], out_dtypes=[F32])
            g, = _matmul("hgrn_g", h, w_in, w_lead=(l,), n_cols=d, col_offsets=(3 * d,), tm=grp.tm_wide, tn_pref=tn, tk_pref=d,
                         epilogue=lambda accs, ex: [accs[0]], out_dtypes=[F32])
            o, s_new = hgrn_fn(l, q, f, v, g, wts["hgrn_out_norm_g"][l])
            states.append(s_new)
            x = _proj_residual("hgrn_out", grp, o, wts["hgrn_w_out"], (l,), x, mod, 5)
        else:
            j = l - n_a
            qgain = wts["sb_q_norm_g"][j].reshape(1, HEAD)
            q, = _matmul("sb_q", h, wts["sb_w_q"], w_lead=(j,), n_cols=d, tm=grp.tm_wide, tn_pref=tn, tk_pref=d,
                         epilogue=lambda accs, ex: [_head_rms(accs[0], ex[0])], out_dtypes=[BF16],
                         extras=[(qgain, (1, HEAD), lambda i, j, k: (0, 0))])
            o = attn_fn(q, k_bf, v_bf, wts["sb_logit_bias"][j])
            x = _proj_residual("sb_out", grp, o, wts["sb_w_o"], (j,), x, mod, 5)
        x = _ffn(grp, x, mod, (6, 7, 8), wts["norm_g"][l, 2], wts["ffn_w_up"], wts["ffn_w_down"], (l, 1), d_ff)
    return x, jnp.stack(states), k_new, v_new


def _modulation(c_all, w, b, w_lead):
    n = b.shape[-1]
    out, = _matmul("adaln", c_all, w, n_cols=n, w_lead=w_lead, tm=c_all.shape[0], tn_pref=512,
                   tk_pref=c_all.shape[1], prologue=_silu,
                   epilogue=lambda accs, ex: [accs[0] + ex[0]], out_dtypes=[F32],
                   extras=[_row_extra(b.reshape(1, n), _pick_tile(n, 512, HEAD))])
    return out


def kernel(x_prompt, x_sample, state_hgrn, cache_k, cache_v, page_table, c_prompt, c_sample,
           w_ada, b_ada, norm_g, ffn_w_up, ffn_w_down, hgrn_w_in, hgrn_lb, hgrn_out_norm_g, hgrn_w_out,
           kv_ada_w, kv_ada_b, kv_norm_g, w_kv, k_norm_g, sb_w_q, sb_q_norm_g, sb_logit_bias, sb_w_o):
    batch, seq, d = x_prompt.shape
    dec_batch, dec_seq, _ = x_sample.shape
    assert dec_seq == 1 and d % HEAD == 0
    heads = d // HEAD
    depth = w_ada.shape[0]
    srows = -(-dec_batch // ROWS_BF16) * ROWS_BF16

    n_c = batch + dec_batch
    crows = -(-n_c // ROWS_BF16) * ROWS_BF16
    c_all = jnp.concatenate([c_prompt, c_sample, jnp.zeros((crows - n_c, d), F32)], axis=0)
    mod_all = [_modulation(c_all, w_ada, b_ada[l], (l,)) for l in range(depth)]
    kv_all = _modulation(c_all, kv_ada_w, kv_ada_b, ())

    def split(mod):
        pad = jnp.zeros((srows - dec_batch, mod.shape[1]), F32)
        return mod[:batch][:, None, :], jnp.concatenate([mod[batch:n_c], pad], axis=0)[None]

    mods_p, mods_s = zip(*[split(m) for m in mod_all])
    kv_p, kv_s = split(kv_all)

    wts = dict(
        norm_g=norm_g, ffn_w_up=ffn_w_up, ffn_w_down=ffn_w_down.astype(BF16),
        hgrn_w_in=hgrn_w_in, hgrn_out_norm_g=hgrn_out_norm_g, hgrn_w_out=hgrn_w_out,
        kv_norm_g=kv_norm_g, w_kv=w_kv, k_norm_g=k_norm_g, sb_w_q=sb_w_q,
        sb_q_norm_g=sb_q_norm_g, sb_logit_bias=sb_logit_bias, sb_w_o=sb_w_o,
        lower_bounds=jnp.cumsum(jax.nn.softmax(hgrn_lb.astype(F32), axis=0), axis=0),
    )

    grp_p = _Group(rows=batch * seq, seq=seq, tm=_pick_tile(seq, 1024, ROWS_BF16),
                   tm_wide=_pick_tile(seq, 2048, ROWS_BF16), tn=256, tr=_pick_tile(seq, 256, ROWS_BF16))
    y_p, st_p, k_p, v_p = _trunk(
        grp_p, x_prompt.reshape(batch * seq, d), mods_p, kv_p, wts,
        lambda l, q, f, v, g, gain: _hgrn_prompt(q, f, v, g, gain, batch, seq),
        lambda q, k, v, bias: _sb_prompt(q, k, v, bias, batch, seq))

    grp_s = _Group(rows=srows, seq=srows, tm=srows, tm_wide=srows, tn=512, tr=srows)
    x_s = jnp.concatenate([x_sample.reshape(dec_batch, d), jnp.zeros((srows - dec_batch, d), F32)], axis=0)

    def attn_sample(q, k_bf, v_bf, bias):
        o = _sb_sample(q[:dec_batch].reshape(dec_batch, heads, HEAD), cache_k, cache_v, page_table, bias)
        return jnp.concatenate([o.reshape(dec_batch, d), jnp.zeros((srows - dec_batch, d), F32)], axis=0)

    y_s, st_s, k_s, v_s = _trunk(
        grp_s, x_s, mods_s, kv_s, wts,
        lambda l, q, f, v, g, gain: _hgrn_step(q, f, v, g, gain, state_hgrn[l]),
        attn_sample)

    return (y_p.reshape(batch, seq, d), y_s[:dec_batch].reshape(dec_batch, 1, d), st_p, st_s,
            k_p.reshape(batch, seq, heads, HEAD), v_p.reshape(batch, seq, heads, HEAD),
            k_s[:dec_batch].reshape(dec_batch, 1, heads, HEAD), v_s[:dec_batch].reshape(dec_batch, 1, heads, HEAD))
```

```python
import functools
import math
from typing import NamedTuple

import jax
import jax.numpy as jnp
from jax import lax
from jax.experimental import pallas as pl
from jax.experimental.pallas import tpu as pltpu

F32 = jnp.float32
BF16 = jnp.bfloat16
EPS = 1e-6
HEAD = 128
SUB = 8
ROWS_BF16 = 16
HGRN_CHUNK = 64
HGRN_UNROLL = 4
SB_HEADS_PER_STEP = 4
SB_KEY_BLOCK = 256
SB_PAGES_PER_STEP = 4
HGRN_FACTOR_MIN_F = math.exp(-70.0)
VMEM_LIMIT = 56 * 1024 * 1024

_NT = (((1,), (1,)), ((), ()))
_TN = (((0,), (0,)), ((), ()))


def _pick_tile(n, pref, align):
    if n <= pref:
        return n
    t = (pref // align) * align
    while t >= align:
        if n % t == 0:
            return t
        t -= align
    raise ValueError(f"no tile for {n} (pref {pref}, align {align})")


class _Group(NamedTuple):
    rows: int
    seq: int
    tm: int
    tm_wide: int
    tn: int
    tr: int


def _silu(x):
    return x * (1.0 / (1.0 + jnp.exp(-x)))


def _sigmoid(x):
    return 1.0 / (1.0 + jnp.exp(-x))


def _log_sigmoid_neg(z):
    return -(jnp.maximum(z, 0.0) + jnp.log1p(jnp.exp(-jnp.abs(z))))


def _mm_kernel(*refs, nw, ne, no, nk, epilogue, prologue):
    a_ref = refs[0]
    w_refs = refs[1:1 + nw]
    e_refs = refs[1 + nw:1 + nw + ne]
    o_refs = refs[1 + nw + ne:1 + nw + ne + no]
    acc_refs = refs[1 + nw + ne + no:]
    a = a_ref[...]
    if prologue is not None:
        a = prologue(a)
    a = a.astype(BF16)
    prods = [jnp.dot(a, w_ref[...].astype(BF16), preferred_element_type=F32) for w_ref in w_refs]

    def finish(accs):
        vals = epilogue(accs, [e[...] for e in e_refs])
        for o_ref, v in zip(o_refs, vals):
            o_ref[...] = v.astype(o_ref.dtype)

    if nk == 1:
        finish(prods)
        return
    k = pl.program_id(2)

    @pl.when(k == 0)
    def _():
        for acc, p in zip(acc_refs, prods):
            acc[...] = p

    @pl.when(k > 0)
    def _():
        for acc, p in zip(acc_refs, prods):
            acc[...] += p

    @pl.when(k == nk - 1)
    def _():
        finish([acc[...] for acc in acc_refs])


def _matmul(name, a, w, *, n_cols, col_offsets=(0,), w_lead=(), tm, tn_pref, tk_pref,
            epilogue, out_dtypes, extras=(), prologue=None, a_single_buffer=False):
    m, kdim = a.shape
    tn = _pick_tile(n_cols, tn_pref, HEAD)
    tk = _pick_tile(kdim, tk_pref, HEAD)
    assert m % tm == 0, (m, tm)
    nm, nn, nk = m // tm, n_cols // tn, kdim // tk
    nw = len(col_offsets)
    w_lead = tuple(w_lead)
    assert w.ndim == 2 + len(w_lead)
    w_specs = []
    for off in col_offsets:
        assert off % tn == 0, (off, tn)
        w_specs.append(pl.BlockSpec((None,) * len(w_lead) + (tk, tn), functools.partial(
            lambda i, j, k, ob: w_lead + (k, j + ob), ob=off // tn)))
    a_mode = dict(pipeline_mode=pl.Buffered(1)) if (a_single_buffer and nk == 1) else {}
    in_specs = [pl.BlockSpec((tm, tk), lambda i, j, k: (i, k), **a_mode)] + w_specs
    in_specs += [pl.BlockSpec(bs, im) for _, bs, im in extras]
    out_specs = [pl.BlockSpec((tm, tn), lambda i, j, k: (i, j)) for _ in out_dtypes]
    out_shape = [jax.ShapeDtypeStruct((m, n_cols), dt) for dt in out_dtypes]
    scratch = [pltpu.VMEM((tm, tn), F32) for _ in range(nw)] if nk > 1 else []
    outs = pl.pallas_call(
        functools.partial(_mm_kernel, nw=nw, ne=len(extras), no=len(out_dtypes), nk=nk,
                          epilogue=epilogue, prologue=prologue),
        grid=(nm, nn, nk),
        in_specs=in_specs,
        out_specs=out_specs,
        out_shape=out_shape,
        scratch_shapes=scratch,
        compiler_params=pltpu.CompilerParams(
            dimension_semantics=("parallel", "parallel", "arbitrary"), vmem_limit_bytes=VMEM_LIMIT),
        name=name,
    )(a, *([w] * nw), *[e for e, _, _ in extras])
    return outs


def _mod_extra(grp, mod, piece, d, tn, tm):
    assert d % tn == 0
    r = mod.shape[1]
    per = grp.seq // tm
    ob = piece * d // tn
    return (mod, (None, r, tn), lambda i, j, k: (i // per, 0, j + ob))


def _row_extra(vec, tn):
    return (vec, (1, tn), lambda i, j, k: (0, j))


def _head_rms(x, gain):
    outs = []
    for g in range(x.shape[1] // HEAD):
        xg = x[:, g * HEAD:(g + 1) * HEAD]
        outs.append(xg * lax.rsqrt(jnp.mean(xg * xg, axis=-1, keepdims=True) + EPS) * gain)
    return outs[0] if len(outs) == 1 else jnp.concatenate(outs, axis=1)


def _modnorm_kernel(x_ref, g_ref, sh_ref, sc_ref, o_ref):
    x = x_ref[...]
    h = x * lax.rsqrt(jnp.mean(x * x, axis=-1, keepdims=True) + EPS) * g_ref[...]
    o_ref[...] = (h * (1.0 + sc_ref[...]) + sh_ref[...]).astype(o_ref.dtype)


def _modnorm(grp, x, gain, mod, shift_piece, scale_piece):
    m, d = x.shape
    r = mod.shape[1]
    per = grp.seq // grp.tr
    return pl.pallas_call(
        _modnorm_kernel,
        grid=(m // grp.tr,),
        in_specs=[
            pl.BlockSpec((grp.tr, d), lambda i: (i, 0)),
            pl.BlockSpec((1, d), lambda i: (0, 0)),
            pl.BlockSpec((None, r, d), lambda i: (i // per, 0, shift_piece)),
            pl.BlockSpec((None, r, d), lambda i: (i // per, 0, scale_piece)),
        ],
        out_specs=pl.BlockSpec((grp.tr, d), lambda i: (i, 0)),
        out_shape=jax.ShapeDtypeStruct((m, d), BF16),
        compiler_params=pltpu.CompilerParams(dimension_semantics=("parallel",), vmem_limit_bytes=VMEM_LIMIT),
        name="modnorm",
    )(x, gain.reshape(1, d), mod, mod)


def _hgrn_subchunk(qv, fv, vv, st, row):
    lg = jnp.log(fv)
    kv = 1.0 - fv
    b = lg
    for sh in (1, 2, 4):
        b = b + jnp.where(row >= sh, pltpu.roll(b, sh, axis=0), 0.0)
    blast = b[SUB - 1:SUB, :]
    qt = qv * jnp.exp(b)
    o = lax.dot_general(qt.astype(BF16), st.astype(BF16), _NT, preferred_element_type=F32)
    o = o + jnp.sum(qv * kv, axis=-1, keepdims=True) * vv
    for d in range(1, SUB):
        dec = jnp.where(row >= d, jnp.exp(b - pltpu.roll(b, d, axis=0)), 0.0)
        s = jnp.sum(qv * pltpu.roll(kv, d, axis=0) * dec, axis=-1, keepdims=True)
        o = o + s * pltpu.roll(vv, d, axis=0)
    kt = kv * jnp.exp(blast - b)
    u = lax.dot_general(vv.astype(BF16), kt.astype(BF16), _TN, preferred_element_type=F32)
    return o, jnp.exp(blast) * st + u


def _gated_head_norm(o, gv, gain):
    return o * lax.rsqrt(jnp.mean(o * o, axis=-1, keepdims=True) + EPS) * gain * _silu(gv)


def _hgrn_span(qv, fv, vv, st, lower, lower_ones, chunk):
    n = qv.shape[0]
    ends = [b1 * chunk - 1 for b1 in range(1, n // chunk + 1)]
    lg = jnp.log(fv)
    kv = 1.0 - fv
    b = sum(jnp.dot(lower_ones, part, preferred_element_type=F32) for part in _split3(lg))
    blast = jnp.concatenate([jnp.broadcast_to(b[e:e + 1, :], (chunk, HEAD)) for e in ends], axis=0)
    qt = (qv * jnp.exp(b)).astype(BF16)
    scores = lax.dot_general(qt, (kv * jnp.exp(-b)).astype(BF16), _NT, preferred_element_type=F32)
    scores = jnp.where(lower, scores, 0.0).astype(BF16)
    vb = vv.astype(BF16)
    kp = (kv * jnp.exp(blast - b)).astype(BF16)
    o_intra = jnp.dot(scores, vb, preferred_element_type=F32)
    outs = []
    for c, e in enumerate(ends):
        rows = slice(c * chunk, (c + 1) * chunk)
        outs.append(o_intra[rows] + lax.dot_general(qt[rows], st.astype(BF16), _NT, preferred_element_type=F32))
        u = lax.dot_general(vb[rows], kp[rows], _TN, preferred_element_type=F32)
        st = jnp.exp(b[e:e + 1, :]) * st + u
    return jnp.concatenate(outs, axis=0), st


def _hgrn_kernel(q_ref, f_ref, v_ref, g_ref, gn_ref, o_ref, s_ref, *, seq, chunk):
    gain = gn_ref[...]
    zero_state = jnp.zeros((HEAD, HEAD), F32)
    fmin = jnp.min(f_ref[...])
    factorable = fmin >= HGRN_FACTOR_MIN_F ** (1.0 / chunk)

    @pl.when(factorable)
    def _():
        span = chunk * HGRN_UNROLL
        t_io = lax.broadcasted_iota(jnp.int32, (span, span), 0)
        s_io = lax.broadcasted_iota(jnp.int32, (span, span), 1)
        shift = chunk.bit_length() - 1
        lower = (s_io <= t_io) & ((s_io >> shift) == (t_io >> shift))
        lower_ones = jnp.where(lower, 1.0, 0.0).astype(BF16)

        def body(n, st):
            rows = pl.ds(pl.multiple_of(n * span, span), span)
            o, st = _hgrn_span(q_ref[rows, :], f_ref[rows, :], v_ref[rows, :], st, lower, lower_ones, chunk)
            o_ref[rows, :] = _gated_head_norm(o, g_ref[rows, :], gain).astype(o_ref.dtype)
            return st

        s_ref[...] = lax.fori_loop(0, seq // span, body, zero_state).T

    @pl.when(jnp.logical_not(factorable))
    def _():
        row = lax.broadcasted_iota(jnp.int32, (SUB, HEAD), 0)

        def body(n, st):
            r0 = pl.multiple_of(n * ROWS_BF16, ROWS_BF16)
            ys = []
            for half in range(ROWS_BF16 // SUB):
                rows = pl.ds(r0 + half * SUB, SUB)
                o, st = _hgrn_subchunk(q_ref[rows, :], f_ref[rows, :], v_ref[rows, :], st, row)
                ys.append(_gated_head_norm(o, g_ref[rows, :], gain))
            o_ref[pl.ds(r0, ROWS_BF16), :] = jnp.concatenate(ys, axis=0).astype(o_ref.dtype)
            return st

        s_ref[...] = lax.fori_loop(0, seq // ROWS_BF16, body, zero_state).T


def _hgrn_prompt(q, f, v, g, gain, batch, seq):
    m, d = q.shape
    heads = d // HEAD
    chunk = _pick_tile(seq, HGRN_CHUNK, ROWS_BF16)
    assert seq % ROWS_BF16 == 0 and (seq // chunk) % HGRN_UNROLL == 0 and chunk & (chunk - 1) == 0
    spec = pl.BlockSpec((seq, HEAD), lambda b, h: (b, h))
    return pl.pallas_call(
        functools.partial(_hgrn_kernel, seq=seq, chunk=chunk),
        grid=(batch, heads),
        in_specs=[spec, spec, spec, spec, pl.BlockSpec((1, HEAD), lambda b, h: (0, 0))],
        out_specs=[spec, pl.BlockSpec((None, None, HEAD, HEAD), lambda b, h: (b, h, 0, 0))],
        out_shape=[jax.ShapeDtypeStruct((m, d), BF16), jax.ShapeDtypeStruct((batch, heads, HEAD, HEAD), F32)],
        compiler_params=pltpu.CompilerParams(
            dimension_semantics=("parallel", "parallel"), vmem_limit_bytes=VMEM_LIMIT),
        name="hgrn_prompt",
    )(q, f, v, g, gain.reshape(1, HEAD))


def _hgrn_step_kernel(q_ref, f_ref, v_ref, g_ref, gn_ref, s0_ref, o_ref, s_ref, *, batch):
    gain = gn_ref[...]
    o_ref[...] = jnp.zeros_like(o_ref)

    def col(rowvec):
        return jnp.broadcast_to(rowvec, (HEAD, HEAD)).T

    for b in range(batch):
        qv, fv, vv, gv = (r[b:b + 1, :] for r in (q_ref, f_ref, v_ref, g_ref))
        s_new = col(fv) * s0_ref[b] + col(1.0 - fv) * vv
        s_ref[b] = s_new
        o = jnp.sum(col(qv) * s_new, axis=0, keepdims=True)
        o_ref[b:b + 1, :] = _gated_head_norm(o, gv, gain)


def _hgrn_step(q, f, v, g, gain, s0):
    rows, d = q.shape
    batch, heads = s0.shape[:2]
    spec = pl.BlockSpec((rows, HEAD), lambda h: (0, h))
    sspec = pl.BlockSpec((batch, None, HEAD, HEAD), lambda h: (0, h, 0, 0))
    return pl.pallas_call(
        functools.partial(_hgrn_step_kernel, batch=batch),
        grid=(heads,),
        in_specs=[spec, spec, spec, spec, pl.BlockSpec((1, HEAD), lambda h: (0, 0)), sspec],
        out_specs=[spec, sspec],
        out_shape=[jax.ShapeDtypeStruct((rows, d), F32), jax.ShapeDtypeStruct(s0.shape, F32)],
        compiler_params=pltpu.CompilerParams(dimension_semantics=("parallel",), vmem_limit_bytes=VMEM_LIMIT),
        name="hgrn_step",
    )(q, f, v, g, gain.reshape(1, HEAD), s0)


def _split3(x):
    hi = x.astype(BF16)
    r1 = x - hi.astype(F32)
    mid = r1.astype(BF16)
    lo = (r1 - mid.astype(F32)).astype(BF16)
    return hi, mid, lo


def _split2(x):
    hi = x.astype(BF16)
    return hi, (x - hi.astype(F32)).astype(BF16)


def _sb_prompt_kernel(q_ref, k_ref, v_ref, bias_ref, o_ref, acc_ref, run_ref, *, seq, scale, tq, hb):
    tb = min(SB_KEY_BLOCK, tq)
    nsub = tq // tb
    later = (lax.broadcasted_iota(jnp.int32, (tb, tb), 0)
             > lax.broadcasted_iota(jnp.int32, (tb, tb), 1)).astype(BF16)

    def step(hh, q0, lo, j, diagonal):
        n = tq - lo
        cols = slice(hh * HEAD, (hh + 1) * HEAD)
        krows = pl.ds(pl.multiple_of(j * tb, tb), tb)
        qb = q_ref[pl.ds(pl.multiple_of(q0 + lo, tb), n), cols]
        z = lax.dot_general(qb, k_ref[krows, cols], _NT, preferred_element_type=F32) * scale
        z = z + bias_ref[:, hh * tb:(hh + 1) * tb]
        lk = _log_sigmoid_neg(z)
        if diagonal:
            causal = lax.broadcasted_iota(jnp.int32, (n, tb), 1) < lax.broadcasted_iota(jnp.int32, (n, tb), 0)
            lkm = jnp.where(causal, lk, 0.0)
        else:
            lkm = lk
        tail = sum(jnp.dot(part, later, preferred_element_type=F32) for part in _split2(lkm))
        w = jnp.exp(z + lk + tail + run_ref[hh, lo:, :])
        if diagonal:
            w = jnp.where(causal, w, 0.0)
        acc_ref[hh, lo:, :] += jnp.dot(w.astype(BF16), v_ref[krows, cols], preferred_element_type=F32)
        run_ref[hh, lo:, :] += jnp.sum(lkm, axis=-1, keepdims=True)

    def q_tile(r, carry):
        q0 = pl.multiple_of(r * tq, tq)
        acc_ref[...] = jnp.zeros_like(acc_ref)
        run_ref[...] = jnp.zeros_like(run_ref)
        for s in reversed(range(nsub)):
            for hh in range(hb):
                step(hh, q0, s * tb, r * nsub + s, True)

        def older(jj, c):
            for hh in range(hb):
                step(hh, q0, 0, r * nsub - 1 - jj, False)
            return c

        lax.fori_loop(0, r * nsub, older, 0)
        for hh in range(hb):
            o_ref[pl.ds(q0, tq), hh * HEAD:(hh + 1) * HEAD] = acc_ref[hh].astype(o_ref.dtype)
        return carry

    lax.fori_loop(0, seq // tq, q_tile, 0)


def _sb_prompt(q, k, v, bias, batch, seq):
    m, d = q.shape
    heads = d // HEAD
    tq = _pick_tile(seq, 512, HEAD)
    hb = _pick_tile(heads, SB_HEADS_PER_STEP, 1)
    assert seq % tq == 0
    spec = pl.BlockSpec((seq, hb * HEAD), lambda b, h: (b, h))
    tb = min(SB_KEY_BLOCK, tq)
    bias_row = jnp.repeat(bias.astype(F32), tb).reshape(1, heads * tb)
    return pl.pallas_call(
        functools.partial(_sb_prompt_kernel, seq=seq, scale=HEAD ** -0.5, tq=tq, hb=hb),
        grid=(batch, heads // hb),
        in_specs=[spec, spec, spec, pl.BlockSpec((1, hb * tb), lambda b, h: (0, h))],
        out_specs=spec,
        out_shape=jax.ShapeDtypeStruct((m, d), BF16),
        scratch_shapes=[pltpu.VMEM((hb, tq, HEAD), F32), pltpu.VMEM((hb, tq, tb), F32)],
        compiler_params=pltpu.CompilerParams(
            dimension_semantics=("parallel", "parallel"), vmem_limit_bytes=VMEM_LIMIT),
        name="sb_prompt",
    )(q, k, v, bias_row)


def _sb_sample_kernel(pt_ref, q_ref, *refs, heads, scale, n_steps, pages_per_step):
    k_refs = refs[:pages_per_step]
    v_refs = refs[pages_per_step:2 * pages_per_step]
    bias_ref, o_ref, acc_ref, run_ref = refs[2 * pages_per_step:]
    p = pl.program_id(1)

    @pl.when(p == 0)
    def _():
        acc_ref[...] = jnp.zeros_like(acc_ref)
        run_ref[...] = jnp.zeros_like(run_ref)

    for k_ref, v_ref in zip(k_refs, v_refs):
        _sb_sample_page(q_ref, k_ref, v_ref, bias_ref, acc_ref, run_ref, heads=heads, scale=scale)

    @pl.when(p == n_steps - 1)
    def _():
        o_ref[...] = acc_ref[...]


def _sb_sample_page(q_ref, k_ref, v_ref, bias_ref, acc_ref, run_ref, *, heads, scale):
    rows = k_ref.shape[0]
    groups = rows // HEAD
    shift = heads.bit_length() - 1

    def head_of(lane):
        return lane & (heads - 1)

    own = head_of(lax.broadcasted_iota(jnp.int32, (heads, HEAD), 1)) == lax.broadcasted_iota(
        jnp.int32, (heads, HEAD), 0)
    y = lax.dot_general(q_ref[...], k_ref[...].astype(BF16), _NT, preferred_element_type=F32)
    z = jnp.concatenate(
        [jnp.sum(jnp.where(own, y[:, j * HEAD:(j + 1) * HEAD], 0.0), axis=0, keepdims=True) for j in range(groups)],
        axis=0)
    z = z * scale + bias_ref[...]
    lk = _log_sigmoid_neg(z)

    src = lax.broadcasted_iota(jnp.int32, (HEAD, HEAD), 0)
    dst = lax.broadcasted_iota(jnp.int32, (HEAD, HEAD), 1)
    same_head = head_of(src) == head_of(dst)
    within_later = jnp.where(same_head, jnp.where((src >> shift) > (dst >> shift), 1.0, 0.0), 0.0).astype(BF16)
    same_head = jnp.where(same_head, 1.0, 0.0).astype(BF16)
    group_later = (lax.broadcasted_iota(jnp.int32, (groups, groups), 1)
                   > lax.broadcasted_iota(jnp.int32, (groups, groups), 0)).astype(BF16)
    parts = _split3(lk)
    within = sum(jnp.dot(part, within_later, preferred_element_type=F32) for part in parts)
    gsum = sum(jnp.dot(part, same_head, preferred_element_type=F32) for part in parts)
    beyond = sum(jnp.dot(group_later, part, preferred_element_type=F32) for part in _split3(gsum))
    w = jnp.exp(z + lk + within + beyond + run_ref[...])
    run_ref[...] += jnp.sum(gsum, axis=0, keepdims=True)
    wd = jnp.concatenate(
        [jnp.where(own, jnp.broadcast_to(w[j:j + 1, :], (heads, HEAD)), 0.0).astype(BF16)
         for j in range(groups)], axis=1)
    acc_ref[...] += jnp.dot(wd, v_ref[...].astype(BF16), preferred_element_type=F32)


def _sb_sample(q, cache_k, cache_v, page_table, bias):
    batch, heads, _ = q.shape
    n_phys, page, _, _ = cache_k.shape
    n_pages = page_table.shape[1]
    assert heads & (heads - 1) == 0 and HEAD % heads == 0 and (page * heads) % HEAD == 0
    rows = page * heads
    bias_row = jnp.tile(bias.astype(F32), HEAD // heads).reshape(1, HEAD)
    pps = _pick_tile(n_pages, SB_PAGES_PER_STEP, 1)
    n_steps = n_pages // pps
    kspecs = [pl.BlockSpec((None, rows, HEAD), functools.partial(
        lambda b, p, pt, i: (pt[b * n_pages + n_pages - 1 - (p * pps + i)], 0, 0), i=i)) for i in range(pps)]
    return pl.pallas_call(
        functools.partial(_sb_sample_kernel, heads=heads, scale=HEAD ** -0.5, n_steps=n_steps, pages_per_step=pps),
        grid_spec=pltpu.PrefetchScalarGridSpec(
            num_scalar_prefetch=1,
            grid=(batch, n_steps),
            in_specs=[pl.BlockSpec((None, heads, HEAD), lambda b, p, pt: (b, 0, 0))] + kspecs + kspecs
            + [pl.BlockSpec((1, HEAD), lambda b, p, pt: (0, 0))],
            out_specs=pl.BlockSpec((None, heads, HEAD), lambda b, p, pt: (b, 0, 0)),
            scratch_shapes=[pltpu.VMEM((heads, HEAD), F32), pltpu.VMEM((1, HEAD), F32)],
        ),
        out_shape=jax.ShapeDtypeStruct((batch, heads, HEAD), F32),
        compiler_params=pltpu.CompilerParams(
            dimension_semantics=("parallel", "arbitrary"), vmem_limit_bytes=VMEM_LIMIT),
        name="sb_sample",
    )(page_table.reshape(-1).astype(jnp.int32), q, *([cache_k.reshape(n_phys, rows, HEAD)] * pps),
      *([cache_v.reshape(n_phys, rows, HEAD)] * pps), bias_row)


def _ffn(grp, x, mod, pieces, gain, w_up, w_down, w_lead, d_ff):
    d = x.shape[1]
    h = _modnorm(grp, x, gain, mod, pieces[0], pieces[1])
    act, = _matmul("ffn_up", h, w_up, w_lead=w_lead, n_cols=d_ff, col_offsets=(0, d_ff), tm=grp.tm_wide,
                   tn_pref=256, tk_pref=d, epilogue=lambda accs, ex: [_silu(accs[0]) * accs[1]], out_dtypes=[BF16])
    tn = _pick_tile(d, 512, HEAD)
    out, = _matmul("ffn_down", act, w_down, w_lead=w_lead, n_cols=d, tm=grp.tm, tn_pref=tn, tk_pref=5632,
                   epilogue=lambda accs, ex: [ex[0] + 0.5 * ex[1] * accs[0]], out_dtypes=[F32],
                   extras=[(x, (grp.tm, tn), lambda i, j, k: (i, j)),
                           _mod_extra(grp, mod, pieces[2], d, tn, grp.tm)])
    return out


def _proj_residual(name, grp, a, w, w_lead, x, mod, gate_piece):
    d = x.shape[1]
    tn = _pick_tile(d, grp.tn, HEAD)
    out, = _matmul(name, a, w, w_lead=w_lead, n_cols=d, tm=grp.tm_wide, tn_pref=tn, tk_pref=a.shape[1],
                   epilogue=lambda accs, ex: [ex[0] + ex[1] * accs[0]], out_dtypes=[F32], a_single_buffer=True,
                   extras=[(x, (grp.tm_wide, tn), lambda i, j, k: (i, j)),
                           _mod_extra(grp, mod, gate_piece, d, tn, grp.tm_wide)])
    return out


def _trunk(grp, x, mods, kv_mod, wts, hgrn_fn, attn_fn):
    d = x.shape[1]
    d_ff = wts["ffn_w_down"].shape[2]
    n_a = wts["hgrn_w_in"].shape[0]
    depth = wts["ffn_w_up"].shape[0]
    tn = _pick_tile(d, grp.tn, HEAD)
    states = []
    k_new = v_new = k_bf = v_bf = None
    for l in range(depth):
        mod = mods[l]
        if l == n_a:
            h = _modnorm(grp, x, wts["kv_norm_g"], kv_mod, 0, 1)
            kgain = wts["k_norm_g"].reshape(1, HEAD)
            k_new, k_bf = _matmul(
                "kv_k", h, wts["w_kv"], n_cols=d, tm=grp.tm_wide, tn_pref=tn, tk_pref=d,
                epilogue=lambda accs, ex: [_head_rms(accs[0], ex[0])] * 2, out_dtypes=[F32, BF16],
                a_single_buffer=True, extras=[(kgain, (1, HEAD), lambda i, j, k: (0, 0))])
            v_new, v_bf = _matmul(
                "kv_v", h, wts["w_kv"], n_cols=d, col_offsets=(d,), tm=grp.tm_wide, tn_pref=tn, tk_pref=d,
                epilogue=lambda accs, ex: [accs[0]] * 2, out_dtypes=[F32, BF16], a_single_buffer=True)
        x = _ffn(grp, x, mod, (0, 1, 2), wts["norm_g"][l, 0], wts["ffn_w_up"], wts["ffn_w_down"], (l, 0), d_ff)
        h = _modnorm(grp, x, wts["norm_g"][l, 1], mod, 3, 4)
        if l < n_a:
            w_in = wts["hgrn_w_in"]
            lb = wts["lower_bounds"][l].reshape(1, d)
            q, = _matmul("hgrn_q", h, w_in, w_lead=(l,), n_cols=d, tm=grp.tm_wide, tn_pref=tn, tk_pref=d,
                         epilogue=lambda accs, ex: [_silu(accs[0])], out_dtypes=[F32])
            f, = _matmul("hgrn_f", h, w_in, w_lead=(l,), n_cols=d, col_offsets=(d,), tm=grp.tm_wide, tn_pref=tn, tk_pref=d,
                         epilogue=lambda accs, ex: [ex[0] + (1.0 - ex[0]) * _sigmoid(accs[0])],
                         out_dtypes=[F32], extras=[_row_extra(lb, tn)])
            v, = _matmul("hgrn_i", h, w_in, w_lead=(l,), n_cols=d, col_offsets=(2 * d,), tm=grp.tm_wide, tn_pref=tn, tk_pref=d,
                         epilogue=lambda accs, ex: [accs[0]], out_dtypes=[F32])
            g, = _matmul("hgrn_g", h, w_in, w_lead=(l,), n_cols=d, col_offsets=(3 * d,), tm=grp.tm_wide, tn_pref=tn, tk_pref=d,
                         epilogue=lambda accs, ex: [accs[0]], out_dtypes=[F32])
            o, s_new = hgrn_fn(l, q, f, v, g, wts["hgrn_out_norm_g"][l])
            states.append(s_new)
            x = _proj_residual("hgrn_out", grp, o, wts["hgrn_w_out"], (l,), x, mod, 5)
        else:
            j = l - n_a
            qgain = wts["sb_q_norm_g"][j].reshape(1, HEAD)
            q, = _matmul("sb_q", h, wts["sb_w_q"], w_lead=(j,), n_cols=d, tm=grp.tm_wide, tn_pref=tn, tk_pref=d,
                         epilogue=lambda accs, ex: [_head_rms(accs[0], ex[0])], out_dtypes=[BF16],
                         extras=[(qgain, (1, HEAD), lambda i, j, k: (0, 0))])
            o = attn_fn(q, k_bf, v_bf, wts["sb_logit_bias"][j])
            x = _proj_residual("sb_out", grp, o, wts["sb_w_o"], (j,), x, mod, 5)
        x = _ffn(grp, x, mod, (6, 7, 8), wts["norm_g"][l, 2], wts["ffn_w_up"], wts["ffn_w_down"], (l, 1), d_ff)
    return x, jnp.stack(states), k_new, v_new


def _modulation(c_all, w, b, w_lead):
    n = b.shape[-1]
    out, = _matmul("adaln", c_all, w, n_cols=n, w_lead=w_lead, tm=c_all.shape[0], tn_pref=512,
                   tk_pref=c_all.shape[1], prologue=_silu,
                   epilogue=lambda accs, ex: [accs[0] + ex[0]], out_dtypes=[F32],
                   extras=[_row_extra(b.reshape(1, n), _pick_tile(n, 512, HEAD))])
    return out


def kernel(x_prompt, x_sample, state_hgrn, cache_k, cache_v, page_table, c_prompt, c_sample,
           w_ada, b_ada, norm_g, ffn_w_up, ffn_w_down, hgrn_w_in, hgrn_lb, hgrn_out_norm_g, hgrn_w_out,
           kv_ada_w, kv_ada_b, kv_norm_g, w_kv, k_norm_g, sb_w_q, sb_q_norm_g, sb_logit_bias, sb_w_o):
    batch, seq, d = x_prompt.shape
    dec_batch, dec_seq, _ = x_sample.shape
    assert dec_seq == 1 and d % HEAD == 0
    heads = d // HEAD
    depth = w_ada.shape[0]
    srows = -(-dec_batch // ROWS_BF16) * ROWS_BF16

    n_c = batch + dec_batch
    crows = -(-n_c // ROWS_BF16) * ROWS_BF16
    c_all = jnp.concatenate([c_prompt, c_sample, jnp.zeros((crows - n_c, d), F32)], axis=0)
    mod_all = [_modulation(c_all, w_ada, b_ada[l], (l,)) for l in range(depth)]
    kv_all = _modulation(c_all, kv_ada_w, kv_ada_b, ())

    def split(mod):
        pad = jnp.zeros((srows - dec_batch, mod.shape[1]), F32)
        return mod[:batch][:, None, :], jnp.concatenate([mod[batch:n_c], pad], axis=0)[None]

    mods_p, mods_s = zip(*[split(m) for m in mod_all])
    kv_p, kv_s = split(kv_all)

    wts = dict(
        norm_g=norm_g, ffn_w_up=ffn_w_up, ffn_w_down=ffn_w_down.astype(BF16),
        hgrn_w_in=hgrn_w_in, hgrn_out_norm_g=hgrn_out_norm_g, hgrn_w_out=hgrn_w_out,
        kv_norm_g=kv_norm_g, w_kv=w_kv, k_norm_g=k_norm_g, sb_w_q=sb_w_q,
        sb_q_norm_g=sb_q_norm_g, sb_logit_bias=sb_logit_bias, sb_w_o=sb_w_o,
        lower_bounds=jnp.cumsum(jax.nn.softmax(hgrn_lb.astype(F32), axis=0), axis=0),
    )

    grp_p = _Group(rows=batch * seq, seq=seq, tm=_pick_tile(seq, 1024, ROWS_BF16),
                   tm_wide=_pick_tile(seq, 2048, ROWS_BF16), tn=256, tr=_pick_tile(seq, 256, ROWS_BF16))
    y_p, st_p, k_p, v_p = _trunk(
        grp_p, x_prompt.reshape(batch * seq, d), mods_p, kv_p, wts,
        lambda l, q, f, v, g, gain: _hgrn_prompt(q, f, v, g, gain, batch, seq),
        lambda q, k, v, bias: _sb_prompt(q, k, v, bias, batch, seq))

    grp_s = _Group(rows=srows, seq=srows, tm=srows, tm_wide=srows, tn=512, tr=srows)
    x_s = jnp.concatenate([x_sample.reshape(dec_batch, d), jnp.zeros((srows - dec_batch, d), F32)], axis=0)

    def attn_sample(q, k_bf, v_bf, bias):
        o = _sb_sample(q[:dec_batch].reshape(dec_batch, heads, HEAD), cache_k, cache_v, page_table, bias)
        return jnp.concatenate([o.reshape(dec_batch, d), jnp.zeros((srows - dec_batch, d), F32)], axis=0)

    y_s, st_s, k_s, v_s = _trunk(
        grp_s, x_s, mods_s, kv_s, wts,
        lambda l, q, f, v, g, gain: _hgrn_step(q, f, v, g, gain, state_hgrn[l]),
        attn_sample)

    return (y_p.reshape(batch, seq, d), y_s[:dec_batch].reshape(dec_batch, 1, d), st_p, st_s,
            k_p.reshape(batch, seq, heads, HEAD), v_p.reshape(batch, seq, heads, HEAD),
            k_s[:dec_batch].reshape(dec_batch, 1, heads, HEAD), v_s[:dec_batch].reshape(dec_batch, 1, heads, HEAD))
```
